```python
import jax, jax.numpy as jnp
from jax import lax
import numpy as np

D_MODEL = 2048
BATCH = 16
SEQ = 256
DEPTH = 4
DEC_BATCH = 8
DEC_SEQ = 4096
PAST_LEN = 512

GRID_W = 64
N_MLA_LAYERS = (DEPTH + 1) // 2
N_SG_LAYERS = DEPTH // 2
EPS = 1e-6
CONV_WIDTH = 2048
MLA_HEADS = 16
QK_NOPE = 128
QK_ROPE = 64
V_HEAD = 128
Q_LORA = 512
KV_LORA = 512
ROPE_BASE = 10000.0
Q_BLOCK = 128
MLA_WIDTH = MLA_HEADS * V_HEAD
CHUNK = 128
SG_WIDTH = 4096
SG_GROUPS = 16
EVEN_IN = 4 * CONV_WIDTH + Q_LORA + KV_LORA + QK_ROPE + MLA_WIDTH
EVEN_SPLITS = (CONV_WIDTH, 2 * CONV_WIDTH, 3 * CONV_WIDTH, 4 * CONV_WIDTH,
               4 * CONV_WIDTH + Q_LORA, 4 * CONV_WIDTH + Q_LORA + KV_LORA,
               4 * CONV_WIDTH + Q_LORA + KV_LORA + QK_ROPE)
EVEN_MIX = CONV_WIDTH + MLA_WIDTH
ODD_IN = 3 * SG_WIDTH

kernel_name = 'hybrid_diffusion_conv_mla_sgmlp_step'

F32 = jnp.float32


def rmsnorm(x, g):
    x32 = x.astype(F32)
    y = x32 * lax.rsqrt(jnp.mean(x32 * x32, axis=-1, keepdims=True) + EPS) * g.astype(F32)
    return y.astype(x.dtype)


def layernorm(x, g, b):
    x32 = x.astype(F32)
    mu = jnp.mean(x32, axis=-1, keepdims=True)
    xc = x32 - mu
    var = jnp.mean(xc * xc, axis=-1, keepdims=True)
    return (xc * lax.rsqrt(var + EPS) * g.astype(F32) + b.astype(F32)).astype(x.dtype)


def ada_mod(cond, w, b):
    m = jax.nn.silu(cond) @ w + b
    return jnp.split(m[:, None, :], 3, axis=-1)


def dwconv3(x, w):
    xp = jnp.pad(x, ((0, 0), (1, 1), (0, 0)))
    return xp[:, :-2] * w[0] + xp[:, 1:-1] * w[1] + xp[:, 2:] * w[2]


def axial_rope_tables(n):
    rows = n // GRID_W
    row = jnp.repeat(jnp.arange(rows, dtype=F32), GRID_W)
    col = jnp.tile(jnp.arange(GRID_W, dtype=F32), rows)
    n_freq = QK_ROPE // 4
    inv = ROPE_BASE ** (-jnp.arange(n_freq, dtype=F32) / n_freq)
    ar = row[:, None] * inv
    ac = col[:, None] * inv
    ang = jnp.concatenate([ar, ar, ac, ac], axis=-1)
    return jnp.cos(ang), jnp.sin(ang)


def apply_rope(x, cos, sin):
    xr = x.reshape(x.shape[:-1] + (2, 2, QK_ROPE // 4))
    rot = jnp.stack([-xr[..., 1, :], xr[..., 0, :]], axis=-2).reshape(x.shape)
    return (x.astype(F32) * cos + rot.astype(F32) * sin).astype(x.dtype)


def mla_attend(q_nope, q_pe, k_nope, k_pe, v):
    bsz, n, h, _ = q_nope.shape
    nb = n // Q_BLOCK
    qn = q_nope.reshape(bsz, nb, Q_BLOCK, h, QK_NOPE).transpose(1, 0, 2, 3, 4)
    qp = q_pe.reshape(bsz, nb, Q_BLOCK, h, QK_ROPE).transpose(1, 0, 2, 3, 4)
    scale = (QK_NOPE + QK_ROPE) ** -0.5

    def block(args):
        qn_b, qp_b = args
        s = (jnp.einsum('bqhd,bkhd->bhqk', qn_b, k_nope, preferred_element_type=F32)
             + jnp.einsum('bqhr,bkr->bhqk', qp_b, k_pe, preferred_element_type=F32))
        p = jax.nn.softmax(s * scale, axis=-1)
        return jnp.einsum('bhqk,bkhd->bqhd', p.astype(v.dtype), v)

    o = lax.map(block, (qn, qp))
    return o.transpose(1, 0, 2, 3, 4).reshape(bsz, n, h * V_HEAD)


def even_mixer(h, w_in, conv_w, q_norm_g, w_qb, kv_norm_g, w_kvb, w_out, rope, ctx):
    bsz, n, _ = h.shape
    cb, cc, cx, cg, q_a, ckv_raw, kpe, mg = jnp.split(h @ w_in, EVEN_SPLITS, axis=-1)
    y_conv = cb * dwconv3(cc * cx, conv_w) * jax.nn.silu(cg)
    q = (rmsnorm(q_a, q_norm_g) @ w_qb).reshape(bsz, n, MLA_HEADS, QK_NOPE + QK_ROPE)
    q_nope, q_pe = q[..., :QK_NOPE], q[..., QK_NOPE:]
    ckv = rmsnorm(ckv_raw, kv_norm_g)
    if rope is None:
        kpe_keys = kpe
    else:
        cos, sin = rope
        q_pe = apply_rope(q_pe, cos[:, None, :], sin[:, None, :])
        kpe_keys = apply_rope(kpe, cos, sin)
    if ctx is None:
        ckv_all, kpe_all = ckv, kpe_keys
    else:
        ckv_all = jnp.concatenate([ctx[0], ckv], axis=1)
        kpe_all = jnp.concatenate([ctx[1], kpe_keys], axis=1)
    kv = (ckv_all @ w_kvb).reshape(bsz, ckv_all.shape[1], MLA_HEADS, QK_NOPE + V_HEAD)
    k_nope, v = kv[..., :QK_NOPE], kv[..., QK_NOPE:]
    y_att = mla_attend(q_nope, q_pe, k_nope, kpe_all, v) * jax.nn.silu(mg)
    out = jnp.concatenate([y_conv, y_att], axis=-1) @ w_out
    return out, ckv, kpe


def odd_mixer(h, w_in, ln_g, ln_b, w_s, b_s, w_out):
    bsz, n, _ = h.shape
    u, v, g = jnp.split(h @ w_in, 3, axis=-1)
    v = layernorm(v, ln_g, ln_b)
    vc = v.reshape(bsz, n // CHUNK, CHUNK, SG_GROUPS, SG_WIDTH // SG_GROUPS)
    vs = jnp.einsum('gpq,bcqgd->bcpgd', w_s, vc) + b_s.T[:, :, None]
    y = u * vs.reshape(bsz, n, SG_WIDTH) * jax.nn.silu(g)
    return y @ w_out


def trunk(x, cond, rope, cache_ckv, cache_kpe, norm_g, w_ada, b_ada,
          e_w_in, e_conv_w, e_q_norm_g, e_w_qb, e_kv_norm_g, e_w_kvb, e_w_out,
          o_w_in, o_ln_g, o_ln_b, o_w_s, o_b_s, o_w_out, final_g):
    ckvs, kpes = [], []
    for l in range(DEPTH):
        shift, scale, gate = ada_mod(cond, w_ada[l], b_ada[l])
        h = rmsnorm(x, norm_g[l]) * (1 + scale) + shift
        i = l // 2
        if l % 2 == 0:
            ctx = None if cache_ckv is None else (cache_ckv[:, i], cache_kpe[:, i])
            out, ckv, kpe = even_mixer(h, e_w_in[i], e_conv_w[i], e_q_norm_g[i], e_w_qb[i],
                                       e_kv_norm_g[i], e_w_kvb[i], e_w_out[i], rope, ctx)
            ckvs.append(ckv)
            kpes.append(kpe)
        else:
            out = odd_mixer(h, o_w_in[i], o_ln_g[i], o_ln_b[i], o_w_s[i], o_b_s[i], o_w_out[i])
        x = x + gate * out
    return rmsnorm(x, final_g), ckvs, kpes


def setup_inputs(seed: int = 0) -> dict:
    key = jax.random.key(seed)
    ks = jax.random.split(key, 24)
    nrm = jax.random.normal
    D = D_MODEL
    return {
        'x_prompt': nrm(ks[0], (BATCH, SEQ, D), F32),
        'x_sample': nrm(ks[1], (DEC_BATCH, DEC_SEQ, D), F32),
        'cache_ckv': nrm(ks[2], (DEC_BATCH, N_MLA_LAYERS, PAST_LEN, KV_LORA), F32),
        'cache_kpe': nrm(ks[3], (DEC_BATCH, N_MLA_LAYERS, PAST_LEN, QK_ROPE), F32),
        'c': nrm(ks[4], (DEC_BATCH, D), F32),
        'c_ctx': nrm(ks[5], (D,), F32),
        'norm_g': 1.0 + 0.02 * nrm(ks[6], (DEPTH, D), F32),
        'w_ada': nrm(ks[7], (DEPTH, D, 3 * D), F32) * D ** -0.5,
        'b_ada': 0.02 * nrm(ks[8], (DEPTH, 3 * D), F32),
        'e_w_in': nrm(ks[9], (N_MLA_LAYERS, D, EVEN_IN), F32) * D ** -0.5,
        'e_conv_w': nrm(ks[10], (N_MLA_LAYERS, 3, CONV_WIDTH), F32) * 3 ** -0.5,
        'e_q_norm_g': 1.0 + 0.02 * nrm(ks[11], (N_MLA_LAYERS, Q_LORA), F32),
        'e_w_qb': nrm(ks[12], (N_MLA_LAYERS, Q_LORA, MLA_HEADS * (QK_NOPE + QK_ROPE)), F32) * Q_LORA ** -0.5,
        'e_kv_norm_g': 1.0 + 0.02 * nrm(ks[13], (N_MLA_LAYERS, KV_LORA), F32),
        'e_w_kvb': nrm(ks[14], (N_MLA_LAYERS, KV_LORA, MLA_HEADS * (QK_NOPE + V_HEAD)), F32) * KV_LORA ** -0.5,
        'e_w_out': nrm(ks[15], (N_MLA_LAYERS, EVEN_MIX, D), F32) * EVEN_MIX ** -0.5,
        'o_w_in': nrm(ks[16], (N_SG_LAYERS, D, ODD_IN), F32) * D ** -0.5,
        'o_ln_g': 1.0 + 0.02 * nrm(ks[17], (N_SG_LAYERS, SG_WIDTH), F32),
        'o_ln_b': 0.02 * nrm(ks[18], (N_SG_LAYERS, SG_WIDTH), F32),
        'o_w_s': nrm(ks[19], (N_SG_LAYERS, SG_GROUPS, CHUNK, CHUNK), F32) * CHUNK ** -0.5,
        'o_b_s': 1.0 + 0.1 * nrm(ks[20], (N_SG_LAYERS, SG_GROUPS, CHUNK), F32),
        'o_w_out': nrm(ks[21], (N_SG_LAYERS, SG_WIDTH, D), F32) * SG_WIDTH ** -0.5,
        'final_g': 1.0 + 0.02 * nrm(ks[22], (D,), F32),
    }


def reference(x_prompt, x_sample, cache_ckv, cache_kpe, c, c_ctx, norm_g, w_ada, b_ada,
              e_w_in, e_conv_w, e_q_norm_g, e_w_qb, e_kv_norm_g, e_w_kvb, e_w_out,
              o_w_in, o_ln_g, o_ln_b, o_w_s, o_b_s, o_w_out, final_g):
    y_prompt, ckvs, kpes = trunk(x_prompt, c_ctx[None, :], None, None, None, norm_g, w_ada, b_ada,
                                 e_w_in, e_conv_w, e_q_norm_g, e_w_qb, e_kv_norm_g, e_w_kvb, e_w_out,
                                 o_w_in, o_ln_g, o_ln_b, o_w_s, o_b_s, o_w_out, final_g)
    new_ckv = jnp.stack(ckvs, axis=1)
    new_kpe = jnp.stack(kpes, axis=1)
    rope = axial_rope_tables(x_sample.shape[1])
    y_sample, _, _ = trunk(x_sample, c, rope, cache_ckv, cache_kpe, norm_g, w_ada, b_ada,
                           e_w_in, e_conv_w, e_q_norm_g, e_w_qb, e_kv_norm_g, e_w_kvb, e_w_out,
                           o_w_in, o_ln_g, o_ln_b, o_w_s, o_b_s, o_w_out, final_g)
    return (y_prompt, y_sample, new_ckv, new_kpe)
```

```python
import functools

import jax
import jax.numpy as jnp
from jax import lax
from jax.experimental import pallas as pl
from jax.experimental.pallas import tpu as pltpu

F32 = jnp.float32
BF16 = jnp.bfloat16

EPS = 1e-6
QK_NOPE = 128
QK_ROPE = 64
V_HEAD = 128
HEAD_QK = 256
GRID_W = 64
ROPE_BASE = 10000.0

V7X_VMEM_BYTES = 64 * 1024 * 1024
VMEM_LIMIT = V7X_VMEM_BYTES - 8 * 1024 * 1024
SUBLANES = 8
LANES = 128


def _silu(x):
    return x * jax.nn.sigmoid(x)


def _dot(a, b):
    return jnp.dot(a, b, preferred_element_type=F32)


def _dot_nt(a, b):
    return lax.dot_general(a, b, (((1,), (1,)), ((), ())), preferred_element_type=F32)


def _rms(x, g):
    return x * lax.rsqrt(jnp.mean(x * x, axis=-1, keepdims=True) + EPS) * g


def _params(*sem):
    return pltpu.CompilerParams(dimension_semantics=sem, vmem_limit_bytes=VMEM_LIMIT)


def _ada_kernel(c_ref, w_ref, b_ref, o_ref):
    a = _silu(c_ref[...]).astype(BF16)
    o_ref[0] = _dot(a, w_ref[0].astype(BF16)) + b_ref[0]


def _ada(cond, w_ada, b_ada):
    depth, d, n = w_ada.shape
    rows = cond.shape[0]
    tn = 768 if n % 768 == 0 else n
    return pl.pallas_call(
        _ada_kernel,
        out_shape=jax.ShapeDtypeStruct((depth, rows, n), F32),
        grid=(depth, n // tn),
        in_specs=[
            pl.BlockSpec((rows, d), lambda l, j: (0, 0)),
            pl.BlockSpec((1, d, tn), lambda l, j: (l, 0, j)),
            pl.BlockSpec((1, 1, tn), lambda l, j: (l, 0, j)),
        ],
        out_specs=pl.BlockSpec((1, rows, tn), lambda l, j: (l, 0, j)),
        compiler_params=_params("parallel", "parallel"),
        name="ada_mod",
    )(cond, w_ada, b_ada.reshape(depth, 1, n))


def _norm_mod(x, g, sc, sh):
    return _rms(x, g) * (1.0 + sc) + sh


def _e1_kernel(x_ref, xp_ref, xn_ref, ng_ref, sh_ref, sc_ref, wb_ref, wc_ref, wx_ref, wg_ref, cw_ref,
               h_ref, y_ref, hx_scr, *, tm, seq):
    i = pl.program_id(0)
    j = pl.program_id(1)

    @pl.when(j == 0)
    def _():
        g, sc, sh = ng_ref[...], sc_ref[0], sh_ref[0]
        hm = _norm_mod(x_ref[...], g, sc, sh).astype(BF16)
        h_ref[...] = hm
        hx_scr[0:tm] = hm
        halo = jnp.concatenate([_norm_mod(xp_ref[...], g, sc, sh), _norm_mod(xn_ref[...], g, sc, sh)], axis=0)
        hx_scr[tm:tm + 2 * SUBLANES] = halo.astype(BF16)

    he = hx_scr[...]
    pe = _dot(he, wc_ref[...]) * _dot(he, wx_ref[...])
    p = pe[:tm]
    p_before = pe[tm + SUBLANES - 1:tm + SUBLANES]
    p_after = pe[tm + SUBLANES:tm + SUBLANES + 1]
    hm = h_ref[...]
    cb = _dot(hm, wb_ref[...])
    cg = _dot(hm, wg_ref[...])

    row = lax.broadcasted_iota(jnp.int32, p.shape, 0)
    pos = (i * tm + row) & (seq - 1)
    prev = jnp.where(row == 0, p_before, pltpu.roll(p, 1, 0))
    prev = jnp.where(pos == 0, 0.0, prev)
    nxt = jnp.where(row == tm - 1, p_after, pltpu.roll(p, tm - 1, 0))
    nxt = jnp.where(pos == seq - 1, 0.0, nxt)
    cw = cw_ref[...]
    conv = prev * cw[0:1] + p * cw[1:2] + nxt * cw[2:3]
    y_ref[...] = (cb * conv * _silu(cg)).astype(BF16)


def _e1(x, mod_l, norm_g, w_conv, conv_w, *, tm, tn, seq, row_of):
    n_tok, d = x.shape
    cw = conv_w.shape[1]
    nj = cw // tn
    n8 = n_tok // SUBLANES
    t8 = tm // SUBLANES
    wspec = lambda grp: pl.BlockSpec((d, tn), lambda i, j: (0, grp * nj + j))
    return pl.pallas_call(
        functools.partial(_e1_kernel, tm=tm, seq=seq),
        out_shape=(jax.ShapeDtypeStruct((n_tok, d), BF16), jax.ShapeDtypeStruct((n_tok, cw), BF16)),
        grid=(n_tok // tm, nj),
        in_specs=[
            pl.BlockSpec((tm, d), lambda i, j: (i, 0)),
            pl.BlockSpec((SUBLANES, d), lambda i, j: (jnp.maximum(i * t8 - 1, 0), 0)),
            pl.BlockSpec((SUBLANES, d), lambda i, j: (jnp.minimum((i + 1) * t8, n8 - 1), 0)),
            pl.BlockSpec((1, d), lambda i, j: (0, 0)),
            pl.BlockSpec((1, 1, d), lambda i, j: (row_of(i), 0, 0)),
            pl.BlockSpec((1, 1, d), lambda i, j: (row_of(i), 0, 1)),
            wspec(0), wspec(1), wspec(2), wspec(3),
            pl.BlockSpec((3, tn), lambda i, j: (0, j)),
        ],
        out_specs=(pl.BlockSpec((tm, d), lambda i, j: (i, 0)), pl.BlockSpec((tm, tn), lambda i, j: (i, j))),
        scratch_shapes=[pltpu.VMEM((tm + 2 * SUBLANES, d), BF16)],
        compiler_params=_params("parallel", "arbitrary"),
        name="even_in_conv",
    )(x, x, x, norm_g, mod_l, mod_l, w_conv, w_conv, w_conv, w_conv, conv_w)


def _rope_pair(v, cs):
    t = v * cs
    return t + pltpu.roll(t, QK_ROPE, 1)


def _e2q_kernel(*refs, heads, rope, scale):
    if rope:
        h_ref, wqa_ref, qg_ref, wqb_ref, cs_ref, q_ref = refs
    else:
        h_ref, wqa_ref, qg_ref, wqb_ref, q_ref = refs
    qa = _dot(h_ref[...], wqa_ref[...])
    qn = _rms(qa, qg_ref[...]).astype(BF16)
    for hd in range(heads):
        base = hd * HEAD_QK
        q = _dot(qn, wqb_ref[:, base:base + HEAD_QK])
        pe = q[:, QK_NOPE:]
        if rope:
            pe = _rope_pair(pe, cs_ref[...])
        q_ref[:, base:base + QK_NOPE] = (q[:, :QK_NOPE] * scale).astype(BF16)
        q_ref[:, base + QK_NOPE:base + HEAD_QK] = (pe * scale).astype(BF16)


def _e2q(h, w_qa, q_norm_g, w_qb_ext, cs, *, tm, seq, heads, scale):
    n_tok, d = h.shape
    ql = w_qa.shape[1]
    rope = cs is not None
    in_specs = [
        pl.BlockSpec((tm, d), lambda i: (i, 0)),
        pl.BlockSpec((d, ql), lambda i: (0, 0)),
        pl.BlockSpec((1, ql), lambda i: (0, 0)),
        pl.BlockSpec((ql, heads * HEAD_QK), lambda i: (0, 0)),
    ]
    args = [h, w_qa, q_norm_g, w_qb_ext]
    if rope:
        nb = seq // tm
        in_specs.append(pl.BlockSpec((tm, LANES), lambda i: (i % nb, 0)))
        args.append(cs)
    return pl.pallas_call(
        functools.partial(_e2q_kernel, heads=heads, rope=rope, scale=scale),
        out_shape=jax.ShapeDtypeStruct((n_tok, heads * HEAD_QK), BF16),
        grid=(n_tok // tm,),
        in_specs=in_specs,
        out_specs=pl.BlockSpec((tm, heads * HEAD_QK), lambda i: (i, 0)),
        compiler_params=_params("parallel"),
        name="mla_q_proj",
    )(*args)


def _write_kv(cb, kper, wkvb_ref, k_ref, v_ref, heads):
    for hd in range(heads):
        base = hd * HEAD_QK
        kv = _dot(cb, wkvb_ref[:, base:base + QK_NOPE + V_HEAD])
        k_ref[:, base:base + QK_NOPE] = kv[:, :QK_NOPE].astype(BF16)
        k_ref[:, base + QK_NOPE:base + HEAD_QK] = kper
        v_ref[:, hd * V_HEAD:(hd + 1) * V_HEAD] = kv[:, QK_NOPE:].astype(BF16)


def _e2k_kernel(*refs, heads, rope, emit):
    refs = list(refs)
    h_ref, wckv_ref, kg_ref, wkpe_ref, wkvb_ref = refs[:5]
    rest = refs[5:]
    cs_ref = rest.pop(0) if rope else None
    k_ref, v_ref = rest[:2]
    h = h_ref[...]
    ckv = _rms(_dot(h, wckv_ref[...]), kg_ref[...])
    kp = _dot(h, wkpe_ref[...])
    if emit:
        ckv_ref, kpe_ref = rest[2:]
        ckv_ref[...] = ckv
        kpe_ref[...] = kp[:, :QK_ROPE]
    kk = _rope_pair(kp, cs_ref[...]) if rope else kp
    lane = lax.broadcasted_iota(jnp.int32, kk.shape, 1)
    kper = jnp.where(lane < QK_ROPE, kk, 0.0).astype(BF16)
    _write_kv(ckv.astype(BF16), kper, wkvb_ref, k_ref, v_ref, heads)


def _e2k(h, w_ckv, kv_norm_g, w_kpe2, w_kvb, cs, *, tm, seq, heads, emit):
    n_tok, d = h.shape
    kvl = w_ckv.shape[1]
    rope = cs is not None
    in_specs = [
        pl.BlockSpec((tm, d), lambda i: (i, 0)),
        pl.BlockSpec((d, kvl), lambda i: (0, 0)),
        pl.BlockSpec((1, kvl), lambda i: (0, 0)),
        pl.BlockSpec((d, LANES), lambda i: (0, 0)),
        pl.BlockSpec((kvl, heads * HEAD_QK), lambda i: (0, 0)),
    ]
    args = [h, w_ckv, kv_norm_g, w_kpe2, w_kvb]
    if rope:
        nb = seq // tm
        in_specs.append(pl.BlockSpec((tm, LANES), lambda i: (i % nb, 0)))
        args.append(cs)
    out_shape = [jax.ShapeDtypeStruct((n_tok, heads * HEAD_QK), BF16),
                 jax.ShapeDtypeStruct((n_tok, heads * V_HEAD), BF16)]
    out_specs = [pl.BlockSpec((tm, heads * HEAD_QK), lambda i: (i, 0)),
                 pl.BlockSpec((tm, heads * V_HEAD), lambda i: (i, 0))]
    if emit:
        out_shape += [jax.ShapeDtypeStruct((n_tok, kvl), F32), jax.ShapeDtypeStruct((n_tok, QK_ROPE), F32)]
        out_specs += [pl.BlockSpec((tm, kvl), lambda i: (i, 0)), pl.BlockSpec((tm, QK_ROPE), lambda i: (i, 0))]
    return pl.pallas_call(
        functools.partial(_e2k_kernel, heads=heads, rope=rope, emit=emit),
        out_shape=tuple(out_shape),
        grid=(n_tok // tm,),
        in_specs=in_specs,
        out_specs=tuple(out_specs),
        compiler_params=_params("parallel"),
        name="mla_kv_proj",
    )(*args)


def _cache_kv_kernel(c_ref, kpe_ref, wkvb_ref, k_ref, v_ref, *, heads):
    _write_kv(c_ref[...].astype(BF16), kpe_ref[...], wkvb_ref, k_ref, v_ref, heads)


def _cache_kv(ckv, kpe_pad, w_kvb, *, tm, heads):
    n, kvl = ckv.shape
    return pl.pallas_call(
        functools.partial(_cache_kv_kernel, heads=heads),
        out_shape=(jax.ShapeDtypeStruct((n, heads * HEAD_QK), BF16),
                   jax.ShapeDtypeStruct((n, heads * V_HEAD), BF16)),
        grid=(n // tm,),
        in_specs=[
            pl.BlockSpec((tm, kvl), lambda i: (i, 0)),
            pl.BlockSpec((tm, LANES), lambda i: (i, 0)),
            pl.BlockSpec((kvl, heads * HEAD_QK), lambda i: (0, 0)),
        ],
        out_specs=(pl.BlockSpec((tm, heads * HEAD_QK), lambda i: (i, 0)),
                   pl.BlockSpec((tm, heads * V_HEAD), lambda i: (i, 0))),
        compiler_params=_params("parallel"),
        name="mla_cache_kv",
    )(ckv, kpe_pad, w_kvb)


def _gate_kernel(h_ref, w_ref, o_ref):
    o_ref[...] = _silu(_dot(h_ref[...], w_ref[...])).astype(BF16)


def _gate(h, w, *, tm, tn):
    n_tok, d = h.shape
    n = w.shape[1]
    return pl.pallas_call(
        _gate_kernel,
        out_shape=jax.ShapeDtypeStruct((n_tok, n), BF16),
        grid=(n_tok // tm, n // tn),
        in_specs=[pl.BlockSpec((tm, d), lambda i, j: (i, 0)), pl.BlockSpec((d, tn), lambda i, j: (0, j))],
        out_specs=pl.BlockSpec((tm, tn), lambda i, j: (i, j)),
        compiler_params=_params("parallel", "arbitrary"),
        name="mla_gate",
    )(h, w)


def _attn_kernel(*refs, n_src):
    q_ref = refs[0]
    kv_refs = refs[1:1 + 2 * n_src]
    gate_ref, o_ref = refs[1 + 2 * n_src:]
    q = q_ref[...]
    s = [_dot_nt(q, kv_refs[2 * t][...]) for t in range(n_src)]
    m = functools.reduce(jnp.maximum, [jnp.max(x, axis=-1, keepdims=True) for x in s])
    p = [jnp.exp(x - m) for x in s]
    l = functools.reduce(jnp.add, [jnp.sum(x, axis=-1, keepdims=True) for x in p])
    o = functools.reduce(jnp.add, [_dot(p[t].astype(BF16), kv_refs[2 * t + 1][...]) for t in range(n_src)])
    o_ref[...] = (o / l * gate_ref[...].astype(F32)).astype(BF16)


def _attention(q, srcs, gate, *, batch, seq, heads, tq):
    n_tok = q.shape[0]
    nq = seq // tq
    in_specs = [pl.BlockSpec((tq, HEAD_QK), lambda b, hd, t: (b * nq + t, hd))]
    args = [q]
    for k, v, nk in srcs:
        in_specs.append(pl.BlockSpec((nk, HEAD_QK), lambda b, hd, t: (b, hd)))
        in_specs.append(pl.BlockSpec((nk, V_HEAD), lambda b, hd, t: (b, hd)))
        args += [k, v]
    in_specs.append(pl.BlockSpec((tq, V_HEAD), lambda b, hd, t: (b * nq + t, hd)))
    args.append(gate)
    return pl.pallas_call(
        functools.partial(_attn_kernel, n_src=len(srcs)),
        out_shape=jax.ShapeDtypeStruct((n_tok, heads * V_HEAD), BF16),
        grid=(batch, heads, nq),
        in_specs=in_specs,
        out_specs=pl.BlockSpec((tq, V_HEAD), lambda b, hd, t: (b * nq + t, hd)),
        compiler_params=_params("parallel", "parallel", "arbitrary"),
        name="mla_attention",
    )(*args)


def _out_kernel(y1_ref, y2_ref, w1_ref, w2_ref, x_ref, g_ref, o_ref):
    acc = _dot(y1_ref[...], w1_ref[...]) + _dot(y2_ref[...], w2_ref[...])
    o_ref[...] = x_ref[...] + g_ref[0] * acc


def _out_proj(y1, y2, c1, c2, w_out, x, mod_l, *, tm, tn, row_of):
    n_tok, d = x.shape
    kh = w_out.shape[0] // 2
    return pl.pallas_call(
        _out_kernel,
        out_shape=jax.ShapeDtypeStruct((n_tok, d), F32),
        grid=(n_tok // tm, d // tn),
        in_specs=[
            pl.BlockSpec((tm, kh), lambda i, j: (i, c1)),
            pl.BlockSpec((tm, kh), lambda i, j: (i, c2)),
            pl.BlockSpec((kh, tn), lambda i, j: (0, j)),
            pl.BlockSpec((kh, tn), lambda i, j: (1, j)),
            pl.BlockSpec((tm, tn), lambda i, j: (i, j)),
            pl.BlockSpec((1, 1, tn), lambda i, j: (row_of(i), 0, 2 * (d // tn) + j)),
        ],
        out_specs=pl.BlockSpec((tm, tn), lambda i, j: (i, j)),
        compiler_params=_params("parallel", "arbitrary"),
        name="out_proj",
    )(y1, y2, w_out, w_out, x, mod_l)


def _final_kernel(x_ref, g_ref, o_ref):
    o_ref[...] = _rms(x_ref[...], g_ref[...])


def _final_norm(x, g, *, tm):
    n_tok, d = x.shape
    return pl.pallas_call(
        _final_kernel,
        out_shape=jax.ShapeDtypeStruct((n_tok, d), F32),
        grid=(n_tok // tm,),
        in_specs=[pl.BlockSpec((tm, d), lambda i: (i, 0)), pl.BlockSpec((1, d), lambda i: (0, 0))],
        out_specs=pl.BlockSpec((tm, d), lambda i: (i, 0)),
        compiler_params=_params("parallel"),
        name="final_norm",
    )(x, g)


def _o1a_kernel(x_ref, ng_ref, sh_ref, sc_ref, wv_ref, lg_ref, lb_ref, h_ref, vn_ref, vacc, *, nj, tnv):
    j = pl.program_id(1)

    @pl.when(j == 0)
    def _():
        h_ref[...] = _norm_mod(x_ref[...], ng_ref[...], sc_ref[0], sh_ref[0]).astype(BF16)

    vacc[j] = _dot(h_ref[...], wv_ref[...])

    @pl.when(j == nj - 1)
    def _():
        width = nj * tnv
        tot = functools.reduce(jnp.add, [jnp.sum(vacc[t], axis=-1, keepdims=True) for t in range(nj)])
        mu = tot / width
        sq = functools.reduce(
            jnp.add, [jnp.sum((vacc[t] - mu) * (vacc[t] - mu), axis=-1, keepdims=True) for t in range(nj)])
        inv = lax.rsqrt(sq / width + EPS)
        for t in range(nj):
            sl = slice(t * tnv, (t + 1) * tnv)
            vn_ref[:, sl] = ((vacc[t] - mu) * inv * lg_ref[:, sl] + lb_ref[:, sl]).astype(BF16)


def _o1a(x, mod_l, norm_g, w_in, ln_g, ln_b, *, tm, tnv, row_of):
    n_tok, d = x.shape
    sgw = ln_g.shape[1]
    nj = sgw // tnv
    return pl.pallas_call(
        functools.partial(_o1a_kernel, nj=nj, tnv=tnv),
        out_shape=(jax.ShapeDtypeStruct((n_tok, d), BF16), jax.ShapeDtypeStruct((n_tok, sgw), BF16)),
        grid=(n_tok // tm, nj),
        in_specs=[
            pl.BlockSpec((tm, d), lambda i, j: (i, 0)),
            pl.BlockSpec((1, d), lambda i, j: (0, 0)),
            pl.BlockSpec((1, 1, d), lambda i, j: (row_of(i), 0, 0)),
            pl.BlockSpec((1, 1, d), lambda i, j: (row_of(i), 0, 1)),
            pl.BlockSpec((d, tnv), lambda i, j: (0, nj + j)),
            pl.BlockSpec((1, sgw), lambda i, j: (0, 0)),
            pl.BlockSpec((1, sgw), lambda i, j: (0, 0)),
        ],
        out_specs=(pl.BlockSpec((tm, d), lambda i, j: (i, 0)), pl.BlockSpec((tm, sgw), lambda i, j: (i, 0))),
        scratch_shapes=[pltpu.VMEM((nj, tm, tnv), F32)],
        compiler_params=_params("parallel", "arbitrary"),
        name="odd_in_v_ln",
    )(x, norm_g, mod_l, mod_l, w_in, ln_g, ln_b)


def _o1b_kernel(h_ref, wu_ref, wg_ref, vn_ref, ws_ref, bs_ref, y_ref, *, tm, chunk):
    h = h_ref[...]
    u = _dot(h, wu_ref[...])
    g = _dot(h, wg_ref[...])
    ws = ws_ref[0]
    bs = bs_ref[0]
    for c in range(tm // chunk):
        sl = slice(c * chunk, (c + 1) * chunk)
        vs = _dot(ws, vn_ref[sl, :]) + bs
        y_ref[sl, :] = (u[sl] * vs * _silu(g[sl])).astype(BF16)


def _o1b(h, w_in, vn, w_s, b_s, *, tm):
    n_tok, d = h.shape
    groups, chunk, _ = w_s.shape
    sgw = vn.shape[1]
    tn = sgw // groups
    return pl.pallas_call(
        functools.partial(_o1b_kernel, tm=tm, chunk=chunk),
        out_shape=jax.ShapeDtypeStruct((n_tok, sgw), BF16),
        grid=(n_tok // tm, groups),
        in_specs=[
            pl.BlockSpec((tm, d), lambda i, j: (i, 0)),
            pl.BlockSpec((d, tn), lambda i, j: (0, j)),
            pl.BlockSpec((d, tn), lambda i, j: (0, 2 * groups + j)),
            pl.BlockSpec((tm, tn), lambda i, j: (i, j)),
            pl.BlockSpec((1, chunk, chunk), lambda i, j: (j, 0, 0)),
            pl.BlockSpec((1, chunk, 1), lambda i, j: (j, 0, 0)),
        ],
        out_specs=pl.BlockSpec((tm, tn), lambda i, j: (i, j)),
        compiler_params=_params("parallel", "arbitrary"),
        name="odd_spatial_gate",
    )(h, w_in, w_in, vn, w_s, b_s)


def _rot_cols(w):
    shp = w.shape
    w4 = w.reshape(shp[:-1] + (2, 2, QK_ROPE // 4))
    return jnp.stack([-w4[..., 1, :], w4[..., 0, :]], axis=-2).reshape(shp)


def _rope_table(n):
    rows = n // GRID_W
    row = jnp.repeat(jnp.arange(rows, dtype=F32), GRID_W)
    col = jnp.tile(jnp.arange(GRID_W, dtype=F32), rows)
    n_freq = QK_ROPE // 4
    inv = ROPE_BASE ** (-jnp.arange(n_freq, dtype=F32) / n_freq)
    ar = row[:, None] * inv
    ac = col[:, None] * inv
    ang = jnp.concatenate([ar, ar, ac, ac], axis=-1)
    return jnp.concatenate([jnp.cos(ang), jnp.sin(ang)], axis=-1)


def _tile(n, pref):
    return pref if n % pref == 0 else n


def _prep_even(e_w_in, e_w_qb, e_w_kvb, e_w_out, i, cw, ql, kvl, heads):
    w_in = e_w_in[i]
    o = 4 * cw
    w_kpe = w_in[:, o + ql + kvl:o + ql + kvl + QK_ROPE]
    d = w_in.shape[0]
    wq = e_w_qb[i].reshape(ql, heads, QK_NOPE + QK_ROPE)
    pe = wq[..., QK_NOPE:]
    w_qb_ext = jnp.concatenate([wq[..., :QK_NOPE], pe, _rot_cols(pe)], axis=-1).reshape(ql, heads * HEAD_QK)
    return dict(
        w_conv=w_in[:, :o].astype(BF16),
        w_qa=w_in[:, o:o + ql].astype(BF16),
        w_ckv=w_in[:, o + ql:o + ql + kvl].astype(BF16),
        w_kpe2=jnp.concatenate([w_kpe, _rot_cols(w_kpe)], axis=-1).astype(BF16),
        w_mg=w_in[:, o + ql + kvl + QK_ROPE:].astype(BF16),
        w_qb_ext=w_qb_ext.astype(BF16),
        w_kvb=e_w_kvb[i].astype(BF16),
        w_out=e_w_out[i].astype(BF16),
    )


def _trunk(x, *, batch, seq, row_of, cs, cache, mod, norm_g, even, e_conv_w, e_q_norm_g, e_kv_norm_g,
           odd, o_ln_g, o_ln_b, o_b_s, final_g, heads, emit):
    n_tok, d = x.shape
    depth = mod.shape[0]
    tm = _tile(n_tok, 512)
    scale = float((QK_NOPE + QK_ROPE) ** -0.5)
    ckvs, kpes = [], []
    for l in range(depth):
        mod_l = mod[l].reshape(mod.shape[1], 1, mod.shape[2])
        g_l = norm_g[l][None, :]
        i = l // 2
        if l % 2 == 0:
            w = even[i]
            h, y_conv = _e1(x, mod_l, g_l, w["w_conv"], e_conv_w[i], tm=tm,
                            tn=_tile(e_conv_w.shape[2], 256), seq=seq, row_of=row_of)
            q = _e2q(h, w["w_qa"], e_q_norm_g[i][None, :], w["w_qb_ext"], cs, tm=tm, seq=seq,
                     heads=heads, scale=scale)
            kv = _e2k(h, w["w_ckv"], e_kv_norm_g[i][None, :], w["w_kpe2"], w["w_kvb"], cs, tm=tm, seq=seq,
                      heads=heads, emit=emit)
            if emit:
                ckvs.append(kv[2])
                kpes.append(kv[3])
            gate = _gate(h, w["w_mg"], tm=tm, tn=_tile(w["w_mg"].shape[1], 512))
            srcs = []
            if cache is not None:
                c_ckv, c_kpe = cache
                past = c_ckv.shape[2]
                kvl = c_ckv.shape[3]
                kpe_pad = jnp.pad(c_kpe[:, i].reshape(batch * past, QK_ROPE),
                                  ((0, 0), (0, LANES - QK_ROPE))).astype(BF16)
                kc, vc = _cache_kv(c_ckv[:, i].reshape(batch * past, kvl), kpe_pad, w["w_kvb"],
                                   tm=_tile(batch * past, 512), heads=heads)
                srcs.append((kc, vc, past))
            srcs.append((kv[0], kv[1], seq))
            y_att = _attention(q, srcs, gate, batch=batch, seq=seq, heads=heads, tq=_tile(seq, 512))
            x = _out_proj(y_conv, y_att, 0, 0, w["w_out"], x, mod_l, tm=tm, tn=_tile(d, 512), row_of=row_of)
        else:
            w = odd[i]
            h, vn = _o1a(x, mod_l, g_l, w["w_in"], o_ln_g[i][None, :], o_ln_b[i][None, :], tm=tm,
                         tnv=_tile(o_ln_g.shape[1], 512), row_of=row_of)
            y = _o1b(h, w["w_in"], vn, w["w_s"], o_b_s[i][:, :, None], tm=tm)
            x = _out_proj(y, y, 0, 1, w["w_out"], x, mod_l, tm=tm, tn=_tile(d, 512), row_of=row_of)
    return _final_norm(x, final_g[None, :], tm=tm), ckvs, kpes


def kernel(x_prompt, x_sample, cache_ckv, cache_kpe, c, c_ctx, norm_g, w_ada, b_ada, e_w_in, e_conv_w,
           e_q_norm_g, e_w_qb, e_kv_norm_g, e_w_kvb, e_w_out, o_w_in, o_ln_g, o_ln_b, o_w_s, o_b_s, o_w_out,
           final_g):
    batch, seq, d = x_prompt.shape
    dec_batch, dec_seq, _ = x_sample.shape
    cw = e_conv_w.shape[2]
    ql = e_q_norm_g.shape[1]
    kvl = e_kv_norm_g.shape[1]
    heads = e_w_qb.shape[2] // (QK_NOPE + QK_ROPE)
    n_even, n_odd = e_w_in.shape[0], o_w_in.shape[0]

    rows = -(-(dec_batch + 1) // SUBLANES) * SUBLANES
    cond = jnp.concatenate([c, c_ctx[None, :], jnp.zeros((rows - dec_batch - 1, d), F32)], axis=0)
    mod = _ada(cond, w_ada, b_ada)

    even = [_prep_even(e_w_in, e_w_qb, e_w_kvb, e_w_out, i, cw, ql, kvl, heads) for i in range(n_even)]
    odd = [dict(w_in=o_w_in[i].astype(BF16), w_s=o_w_s[i].astype(BF16), w_out=o_w_out[i].astype(BF16))
           for i in range(n_odd)]
    shared = dict(mod=mod, norm_g=norm_g, even=even, e_conv_w=e_conv_w, e_q_norm_g=e_q_norm_g,
                  e_kv_norm_g=e_kv_norm_g, odd=odd, o_ln_g=o_ln_g, o_ln_b=o_ln_b, o_b_s=o_b_s,
                  final_g=final_g, heads=heads)

    y_prompt, ckvs, kpes = _trunk(x_prompt.reshape(batch * seq, d), batch=batch, seq=seq,
                                  row_of=lambda i: dec_batch, cs=None, cache=None, emit=True, **shared)
    tm_dec = _tile(dec_batch * dec_seq, 512)
    y_sample, _, _ = _trunk(x_sample.reshape(dec_batch * dec_seq, d), batch=dec_batch, seq=dec_seq,
                            row_of=lambda i: (i * tm_dec) // dec_seq, cs=_rope_table(dec_seq),
                            cache=(cache_ckv, cache_kpe), emit=False, **shared)
    new_ckv = jnp.stack([t.reshape(batch, seq, kvl) for t in ckvs], axis=1)
    new_kpe = jnp.stack([t.reshape(batch, seq, QK_ROPE) for t in kpes], axis=1)
    return (y_prompt.reshape(batch, seq, d), y_sample.reshape(dec_batch, dec_seq, d), new_ckv, new_kpe)
```

```python
import functools
import math

import jax
import jax.numpy as jnp
from jax import lax
from jax.experimental import pallas as pl
from jax.experimental.pallas import tpu as pltpu

F32 = jnp.float32
BF16 = jnp.bfloat16

EPS = 1e-6
QK_NOPE = 128
QK_ROPE = 64
V_HEAD = 128
HEAD_QK = 256
HEAD_V = 256
GRID_W = 64
ROPE_BASE = 10000.0

V7X_VMEM_BYTES = 64 * 1024 * 1024
VMEM_LIMIT = V7X_VMEM_BYTES - 8 * 1024 * 1024
SUBLANES = 8
PACKED_ROWS = 16
LANES = 128


def _silu(x):
    return x * jax.nn.sigmoid(x)


def _dot(a, b):
    return jnp.dot(a, b, preferred_element_type=F32)


def _dot_nt(a, b):
    return lax.dot_general(a, b, (((1,), (1,)), ((), ())), preferred_element_type=F32)


def _rms(x, g):
    return x * lax.rsqrt(jnp.mean(x * x, axis=-1, keepdims=True) + EPS) * g


def _params(*sem):
    return pltpu.CompilerParams(dimension_semantics=sem, vmem_limit_bytes=VMEM_LIMIT)


def _ada_kernel(c_ref, w_ref, b_ref, o_ref):
    a = _silu(c_ref[...]).astype(BF16)
    o_ref[0] = _dot(a, w_ref[0].astype(BF16)) + b_ref[0]


def _ada(cond, w_ada, b_ada):
    depth, d, n = w_ada.shape
    rows = cond.shape[0]
    tn = 768 if n % 768 == 0 else n
    return pl.pallas_call(
        _ada_kernel,
        out_shape=jax.ShapeDtypeStruct((depth, rows, n), F32),
        grid=(depth, n // tn),
        in_specs=[
            pl.BlockSpec((rows, d), lambda l, j: (0, 0)),
            pl.BlockSpec((1, d, tn), lambda l, j: (l, 0, j)),
            pl.BlockSpec((1, 1, tn), lambda l, j: (l, 0, j)),
        ],
        out_specs=pl.BlockSpec((1, rows, tn), lambda l, j: (l, 0, j)),
        compiler_params=_params("parallel", "parallel"),
        name="ada_mod",
    )(cond, w_ada, b_ada.reshape(depth, 1, n))


def _norm_mod(x, g, sc, sh):
    return _rms(x, g) * (1.0 + sc) + sh


def _norm_mod_kernel(x_ref, ng_ref, sh_ref, sc_ref, h_ref):
    h_ref[...] = _norm_mod(x_ref[...], ng_ref[...], sc_ref[0], sh_ref[0]).astype(BF16)


def _first_h(x, mod_l, norm_g, *, tm, row_of):
    n_tok, d = x.shape
    return pl.pallas_call(
        _norm_mod_kernel,
        out_shape=jax.ShapeDtypeStruct((n_tok, d), BF16),
        grid=(n_tok // tm,),
        in_specs=[
            pl.BlockSpec((tm, d), lambda i: (i, 0)),
            pl.BlockSpec((1, d), lambda i: (0, 0)),
            pl.BlockSpec((1, 1, d), lambda i: (row_of(i * tm), 0, 0)),
            pl.BlockSpec((1, 1, d), lambda i: (row_of(i * tm), 0, 1)),
        ],
        out_specs=pl.BlockSpec((tm, d), lambda i: (i, 0)),
        compiler_params=_params("parallel"),
        name="norm_mod",
    )(x, norm_g, mod_l, mod_l)


def _conv_kernel(h_ref, hp_ref, hn_ref, wb_ref, wc_ref, wx_ref, wg_ref, cw_ref, y_ref, hx_scr, *, tm, seq):
    i = pl.program_id(0)
    j = pl.program_id(1)

    @pl.when(j == 0)
    def _():
        hx_scr[0:tm] = h_ref[...]
        hx_scr[tm:tm + PACKED_ROWS] = hp_ref[...]
        hx_scr[tm + PACKED_ROWS:tm + 2 * PACKED_ROWS] = hn_ref[...]

    he = hx_scr[...]
    pe = _dot(he, wc_ref[...]) * _dot(he, wx_ref[...])
    p = pe[:tm]
    p_before = pe[tm + PACKED_ROWS - 1:tm + PACKED_ROWS]
    p_after = pe[tm + PACKED_ROWS:tm + PACKED_ROWS + 1]
    hm = h_ref[...]
    cb = _dot(hm, wb_ref[...])
    cg = _dot(hm, wg_ref[...])

    row = lax.broadcasted_iota(jnp.int32, p.shape, 0)
    pos = (i * tm + row) & (seq - 1)
    prev = jnp.where(row == 0, p_before, pltpu.roll(p, 1, 0))
    prev = jnp.where(pos == 0, 0.0, prev)
    nxt = jnp.where(row == tm - 1, p_after, pltpu.roll(p, tm - 1, 0))
    nxt = jnp.where(pos == seq - 1, 0.0, nxt)
    cw = cw_ref[...]
    conv = prev * cw[0:1] + p * cw[1:2] + nxt * cw[2:3]
    y_ref[...] = (cb * conv * _silu(cg)).astype(BF16)


def _conv_branch(h, w_conv, conv_w, *, tm, tn, seq):
    n_tok, d = h.shape
    cw = conv_w.shape[1]
    nj = cw // tn
    n16 = n_tok // PACKED_ROWS
    t16 = tm // PACKED_ROWS
    wspec = lambda grp: pl.BlockSpec((d, tn), lambda i, j: (0, grp * nj + j))
    return pl.pallas_call(
        functools.partial(_conv_kernel, tm=tm, seq=seq),
        out_shape=jax.ShapeDtypeStruct((n_tok, cw), BF16),
        grid=(n_tok // tm, nj),
        in_specs=[
            pl.BlockSpec((tm, d), lambda i, j: (i, 0)),
            pl.BlockSpec((PACKED_ROWS, d), lambda i, j: (jnp.maximum(i * t16 - 1, 0), 0)),
            pl.BlockSpec((PACKED_ROWS, d), lambda i, j: (jnp.minimum((i + 1) * t16, n16 - 1), 0)),
            wspec(0), wspec(1), wspec(2), wspec(3),
            pl.BlockSpec((3, tn), lambda i, j: (0, j)),
        ],
        out_specs=pl.BlockSpec((tm, tn), lambda i, j: (i, j)),
        scratch_shapes=[pltpu.VMEM((tm + 2 * PACKED_ROWS, d), BF16)],
        compiler_params=_params("parallel", "arbitrary"),
        name="even_conv",
    )(h, h, h, w_conv, w_conv, w_conv, w_conv, conv_w)


def _rope_pair(v, cs):
    t = v * cs
    return t + pltpu.roll(t, QK_ROPE, 1)


def _q_kernel(*refs, heads, rope, scale):
    if rope:
        h_ref, wqa_ref, qg_ref, wqb_ref, cs_ref, q_ref = refs
    else:
        h_ref, wqa_ref, qg_ref, wqb_ref, q_ref = refs
    qa = _dot(h_ref[...], wqa_ref[...])
    qn = _rms(qa, qg_ref[...]).astype(BF16)
    for hd in range(heads):
        base = hd * HEAD_QK
        q = _dot(qn, wqb_ref[:, base:base + HEAD_QK])
        pe = q[:, QK_NOPE:]
        if rope:
            pe = _rope_pair(pe, cs_ref[...])
        q_ref[:, base:base + QK_NOPE] = (q[:, :QK_NOPE] * scale).astype(BF16)
        q_ref[:, base + QK_NOPE:base + HEAD_QK] = (pe * scale).astype(BF16)


def _q_proj(h, w_qa, q_norm_g, w_qb_ext, cs, *, tm, seq, heads, scale):
    n_tok, d = h.shape
    ql = w_qa.shape[1]
    rope = cs is not None
    in_specs = [
        pl.BlockSpec((tm, d), lambda i: (i, 0)),
        pl.BlockSpec((d, ql), lambda i: (0, 0)),
        pl.BlockSpec((1, ql), lambda i: (0, 0)),
        pl.BlockSpec((ql, heads * HEAD_QK), lambda i: (0, 0)),
    ]
    args = [h, w_qa, q_norm_g, w_qb_ext]
    if rope:
        nb = seq // tm
        in_specs.append(pl.BlockSpec((tm, LANES), lambda i: (i % nb, 0)))
        args.append(cs)
    return pl.pallas_call(
        functools.partial(_q_kernel, heads=heads, rope=rope, scale=scale),
        out_shape=jax.ShapeDtypeStruct((n_tok, heads * HEAD_QK), BF16),
        grid=(n_tok // tm,),
        in_specs=in_specs,
        out_specs=pl.BlockSpec((tm, heads * HEAD_QK), lambda i: (i, 0)),
        compiler_params=_params("parallel"),
        name="mla_q_proj",
    )(*args)


def _write_kv(cb, kper, wkvb_ref, k_ref, v_ref, heads):
    ones = jnp.ones((cb.shape[0], HEAD_V - V_HEAD), BF16)
    for hd in range(heads):
        base = hd * HEAD_QK
        kv = _dot(cb, wkvb_ref[:, base:base + QK_NOPE + V_HEAD])
        k_ref[:, base:base + QK_NOPE] = kv[:, :QK_NOPE].astype(BF16)
        k_ref[:, base + QK_NOPE:base + HEAD_QK] = kper
        v_ref[:, hd * HEAD_V:hd * HEAD_V + V_HEAD] = kv[:, QK_NOPE:].astype(BF16)
        v_ref[:, hd * HEAD_V + V_HEAD:(hd + 1) * HEAD_V] = ones


def _kv_kernel(*refs, heads, rope, emit):
    refs = list(refs)
    h_ref, wckv_ref, kg_ref, wkpe_ref, wkvb_ref = refs[:5]
    rest = refs[5:]
    cs_ref = rest.pop(0) if rope else None
    k_ref, v_ref = rest[:2]
    h = h_ref[...]
    ckv = _rms(_dot(h, wckv_ref[...]), kg_ref[...])
    kp = _dot(h, wkpe_ref[...])
    if emit:
        ckv_ref, kpe_ref = rest[2:]
        ckv_ref[...] = ckv
        kpe_ref[...] = kp[:, :QK_ROPE]
    kk = _rope_pair(kp, cs_ref[...]) if rope else kp
    lane = lax.broadcasted_iota(jnp.int32, kk.shape, 1)
    kper = jnp.where(lane < QK_ROPE, kk, 0.0).astype(BF16)
    _write_kv(ckv.astype(BF16), kper, wkvb_ref, k_ref, v_ref, heads)


def _kv_proj(h, w_ckv, kv_norm_g, w_kpe2, w_kvb, cs, *, tm, seq, heads, emit):
    n_tok, d = h.shape
    kvl = w_ckv.shape[1]
    rope = cs is not None
    in_specs = [
        pl.BlockSpec((tm, d), lambda i: (i, 0)),
        pl.BlockSpec((d, kvl), lambda i: (0, 0)),
        pl.BlockSpec((1, kvl), lambda i: (0, 0)),
        pl.BlockSpec((d, LANES), lambda i: (0, 0)),
        pl.BlockSpec((kvl, heads * HEAD_QK), lambda i: (0, 0)),
    ]
    args = [h, w_ckv, kv_norm_g, w_kpe2, w_kvb]
    if rope:
        nb = seq // tm
        in_specs.append(pl.BlockSpec((tm, LANES), lambda i: (i % nb, 0)))
        args.append(cs)
    out_shape = [jax.ShapeDtypeStruct((n_tok, heads * HEAD_QK), BF16),
                 jax.ShapeDtypeStruct((n_tok, heads * HEAD_V), BF16)]
    out_specs = [pl.BlockSpec((tm, heads * HEAD_QK), lambda i: (i, 0)),
                 pl.BlockSpec((tm, heads * HEAD_V), lambda i: (i, 0))]
    if emit:
        out_shape += [jax.ShapeDtypeStruct((n_tok, kvl), F32), jax.ShapeDtypeStruct((n_tok, QK_ROPE), F32)]
        out_specs += [pl.BlockSpec((tm, kvl), lambda i: (i, 0)), pl.BlockSpec((tm, QK_ROPE), lambda i: (i, 0))]
    return pl.pallas_call(
        functools.partial(_kv_kernel, heads=heads, rope=rope, emit=emit),
        out_shape=tuple(out_shape),
        grid=(n_tok // tm,),
        in_specs=in_specs,
        out_specs=tuple(out_specs),
        compiler_params=_params("parallel"),
        name="mla_kv_proj",
    )(*args)


def _cache_kv_kernel(c_ref, kpe_ref, wkvb_ref, k_ref, v_ref, *, heads):
    _write_kv(c_ref[...].astype(BF16), kpe_ref[...], wkvb_ref, k_ref, v_ref, heads)


def _cache_kv(ckv, kpe_pad, w_kvb, *, tm, heads):
    n, kvl = ckv.shape
    return pl.pallas_call(
        functools.partial(_cache_kv_kernel, heads=heads),
        out_shape=(jax.ShapeDtypeStruct((n, heads * HEAD_QK), BF16),
                   jax.ShapeDtypeStruct((n, heads * HEAD_V), BF16)),
        grid=(n // tm,),
        in_specs=[
            pl.BlockSpec((tm, kvl), lambda i: (i, 0)),
            pl.BlockSpec((tm, LANES), lambda i: (i, 0)),
            pl.BlockSpec((kvl, heads * HEAD_QK), lambda i: (0, 0)),
        ],
        out_specs=(pl.BlockSpec((tm, heads * HEAD_QK), lambda i: (i, 0)),
                   pl.BlockSpec((tm, heads * HEAD_V), lambda i: (i, 0))),
        compiler_params=_params("parallel"),
        name="mla_cache_kv",
    )(ckv, kpe_pad, w_kvb)


def _gate_kernel(h_ref, w_ref, o_ref, *, tn):
    h = h_ref[...]
    for c in range(o_ref.shape[1] // tn):
        sl = slice(c * tn, (c + 1) * tn)
        o_ref[:, sl] = _silu(_dot(h, w_ref[:, sl])).astype(BF16)


def _gate(h, w, *, tm, tn):
    n_tok, d = h.shape
    n = w.shape[1]
    return pl.pallas_call(
        functools.partial(_gate_kernel, tn=tn),
        out_shape=jax.ShapeDtypeStruct((n_tok, n), BF16),
        grid=(n_tok // tm,),
        in_specs=[pl.BlockSpec((tm, d), lambda i: (i, 0)), pl.BlockSpec((d, n), lambda i: (0, 0))],
        out_specs=pl.BlockSpec((tm, n), lambda i: (i, 0)),
        compiler_params=_params("parallel"),
        name="mla_gate",
    )(h, w)


def _finish(acc, gate_ref, o_ref):
    o_ref[...] = (acc[:, :V_HEAD] / acc[:, V_HEAD:] * gate_ref[...].astype(F32)).astype(BF16)


def _attn_kernel(q_ref, k_ref, v_ref, gate_ref, o_ref):
    s = _dot_nt(q_ref[...], k_ref[...])
    p = jnp.exp2(s - jnp.max(s, axis=-1, keepdims=True))
    _finish(_dot(p.astype(BF16), v_ref[...]), gate_ref, o_ref)


def _attention(q, k, v, gate, *, batch, seq, heads, tq):
    n_tok = q.shape[0]
    nq = seq // tq
    return pl.pallas_call(
        _attn_kernel,
        out_shape=jax.ShapeDtypeStruct((n_tok, heads * V_HEAD), BF16),
        grid=(batch, heads, nq),
        in_specs=[
            pl.BlockSpec((tq, HEAD_QK), lambda b, hd, t: (b * nq + t, hd)),
            pl.BlockSpec((seq, HEAD_QK), lambda b, hd, t: (b, hd)),
            pl.BlockSpec((seq, HEAD_V), lambda b, hd, t: (b, hd)),
            pl.BlockSpec((tq, V_HEAD), lambda b, hd, t: (b * nq + t, hd)),
        ],
        out_specs=pl.BlockSpec((tq, V_HEAD), lambda b, hd, t: (b * nq + t, hd)),
        compiler_params=_params("parallel", "parallel", "arbitrary"),
        name="mla_attention",
    )(q, k, v, gate)


def _attn_pipe_kernel(q_ref, kc_ref, ko_ref, vc_ref, vo_ref, gate_ref, o_ref, s_scr, m_scr, *, kc):
    @pl.when(jnp.logical_and(pl.program_id(0) == 0, pl.program_id(1) == 0))
    def _():
        s_scr[...] = jnp.zeros_like(s_scr)
        m_scr[...] = jnp.zeros_like(m_scr)

    q = q_ref[...]
    m_old = m_scr[...]
    acc = None
    m_acc = None
    off = 0
    for k_ref, v_ref in ((kc_ref, vc_ref), (ko_ref, vo_ref)):
        for c in range(k_ref.shape[0] // kc):
            rows = slice(c * kc, (c + 1) * kc)
            cols = slice(off, off + kc)
            s_old = s_scr[:, cols]
            p = jnp.concatenate(
                [jnp.exp2(s_old[:, t * LANES:(t + 1) * LANES] - m_old) for t in range(kc // LANES)], axis=1)
            pv = _dot(p.astype(BF16), v_ref[rows, :])
            acc = pv if acc is None else acc + pv
            s_new = _dot_nt(q, k_ref[rows, :])
            s_scr[:, cols] = s_new
            for t in range(kc // LANES):
                blk = s_new[:, t * LANES:(t + 1) * LANES]
                m_acc = blk if m_acc is None else jnp.maximum(m_acc, blk)
            off += kc
    m_scr[...] = jnp.broadcast_to(jnp.max(m_acc, axis=-1, keepdims=True), m_scr.shape)
    _finish(acc, gate_ref, o_ref)


def _attention_pipelined(q, kc, vc, ko, vo, gate, *, batch, seq, past, heads, tq, kc_size):
    n_tok = q.shape[0]
    nq = seq // tq
    jobs = heads * nq
    jq = lambda g: jnp.minimum(g, jobs - 1)
    jp = lambda g: jnp.maximum(g - 1, 0)
    return pl.pallas_call(
        functools.partial(_attn_pipe_kernel, kc=kc_size),
        out_shape=jax.ShapeDtypeStruct((n_tok, heads * V_HEAD), BF16),
        grid=(batch, jobs + 1),
        in_specs=[
            pl.BlockSpec((tq, HEAD_QK), lambda b, g: (b * nq + jq(g) % nq, jq(g) // nq)),
            pl.BlockSpec((past, HEAD_QK), lambda b, g: (b, jq(g) // nq)),
            pl.BlockSpec((seq, HEAD_QK), lambda b, g: (b, jq(g) // nq)),
            pl.BlockSpec((past, HEAD_V), lambda b, g: (b, jp(g) // nq)),
            pl.BlockSpec((seq, HEAD_V), lambda b, g: (b, jp(g) // nq)),
            pl.BlockSpec((tq, V_HEAD), lambda b, g: (b * nq + jp(g) % nq, jp(g) // nq)),
        ],
        out_specs=pl.BlockSpec((tq, V_HEAD), lambda b, g: (b * nq + jp(g) % nq, jp(g) // nq)),
        scratch_shapes=[pltpu.VMEM((tq, past + seq), F32), pltpu.VMEM((tq, LANES), F32)],
        compiler_params=_params("arbitrary", "arbitrary"),
        name="mla_attention_pipelined",
    )(q, kc, ko, vc, vo, gate)


def _out_kernel(*refs, tn, last):
    if last:
        y1_ref, y2_ref, w1_ref, w2_ref, x_ref, g_ref, ng_ref, o_ref = refs
    else:
        y1_ref, y2_ref, w1_ref, w2_ref, x_ref, g_ref, ng_ref, sh_ref, sc_ref, o_ref, h_ref = refs
    d = x_ref.shape[1]
    y1 = y1_ref[...]
    y2 = y2_ref[...]
    g = g_ref[0]
    ss = None
    for c in range(d // tn):
        sl = slice(c * tn, (c + 1) * tn)
        acc = _dot(y1, w1_ref[:, sl]) + _dot(y2, w2_ref[:, sl])
        xn = x_ref[:, sl] + g[:, sl] * acc
        o_ref[:, sl] = xn
        part = jnp.sum(xn * xn, axis=-1, keepdims=True)
        ss = part if ss is None else ss + part
    inv = lax.rsqrt(ss / d + EPS)
    for c in range(d // tn):
        sl = slice(c * tn, (c + 1) * tn)
        normed = o_ref[:, sl] * inv * ng_ref[:, sl]
        if last:
            o_ref[:, sl] = normed
        else:
            h_ref[:, sl] = (normed * (1.0 + sc_ref[0][:, sl]) + sh_ref[0][:, sl]).astype(BF16)


def _out_proj(y1, y2, c1, c2, w1, w2, x, mod_l, next_g, mod_next, *, tm, tn, row_of):
    n_tok, d = x.shape
    kh = w1.shape[0]
    last = mod_next is None
    in_specs = [
        pl.BlockSpec((tm, kh), lambda i: (i, c1)),
        pl.BlockSpec((tm, kh), lambda i: (i, c2)),
        pl.BlockSpec((kh, d), lambda i: (0, 0)),
        pl.BlockSpec((kh, d), lambda i: (0, 0)),
        pl.BlockSpec((tm, d), lambda i: (i, 0)),
        pl.BlockSpec((1, 1, d), lambda i: (row_of(i * tm), 0, 2)),
        pl.BlockSpec((1, d), lambda i: (0, 0)),
    ]
    args = [y1, y2, w1, w2, x, mod_l, next_g]
    out_shape = [jax.ShapeDtypeStruct((n_tok, d), F32)]
    out_specs = [pl.BlockSpec((tm, d), lambda i: (i, 0))]
    if not last:
        in_specs += [pl.BlockSpec((1, 1, d), lambda i: (row_of(i * tm), 0, 0)),
                     pl.BlockSpec((1, 1, d), lambda i: (row_of(i * tm), 0, 1))]
        args += [mod_next, mod_next]
        out_shape.append(jax.ShapeDtypeStruct((n_tok, d), BF16))
        out_specs.append(pl.BlockSpec((tm, d), lambda i: (i, 0)))
    res = pl.pallas_call(
        functools.partial(_out_kernel, tn=tn, last=last),
        out_shape=tuple(out_shape),
        grid=(n_tok // tm,),
        in_specs=in_specs,
        out_specs=tuple(out_specs),
        compiler_params=_params("parallel"),
        name="out_proj",
    )(*args)
    return res[0] if last else res


def _v_ln_kernel(h_ref, wv_ref, lg_ref, lb_ref, vn_ref, vacc, *, nj, tnv):
    j = pl.program_id(1)
    vacc[j] = _dot(h_ref[...], wv_ref[...])

    @pl.when(j == nj - 1)
    def _():
        width = nj * tnv
        tot = functools.reduce(jnp.add, [jnp.sum(vacc[t], axis=-1, keepdims=True) for t in range(nj)])
        mu = tot / width
        sq = functools.reduce(
            jnp.add, [jnp.sum((vacc[t] - mu) * (vacc[t] - mu), axis=-1, keepdims=True) for t in range(nj)])
        inv = lax.rsqrt(sq / width + EPS)
        for t in range(nj):
            sl = slice(t * tnv, (t + 1) * tnv)
            vn_ref[:, sl] = ((vacc[t] - mu) * inv * lg_ref[:, sl] + lb_ref[:, sl]).astype(BF16)


def _v_ln(h, w_in, ln_g, ln_b, *, tm, tnv):
    n_tok, d = h.shape
    sgw = ln_g.shape[1]
    nj = sgw // tnv
    return pl.pallas_call(
        functools.partial(_v_ln_kernel, nj=nj, tnv=tnv),
        out_shape=jax.ShapeDtypeStruct((n_tok, sgw), BF16),
        grid=(n_tok // tm, nj),
        in_specs=[
            pl.BlockSpec((tm, d), lambda i, j: (i, 0)),
            pl.BlockSpec((d, tnv), lambda i, j: (0, nj + j)),
            pl.BlockSpec((1, sgw), lambda i, j: (0, 0)),
            pl.BlockSpec((1, sgw), lambda i, j: (0, 0)),
        ],
        out_specs=pl.BlockSpec((tm, sgw), lambda i, j: (i, 0)),
        scratch_shapes=[pltpu.VMEM((nj, tm, tnv), F32)],
        compiler_params=_params("parallel", "arbitrary"),
        name="odd_v_ln",
    )(h, w_in, ln_g, ln_b)


def _sg_kernel(h_ref, wu_ref, wg_ref, vn_ref, ws_ref, bs_ref, y_ref, *, tm, chunk):
    h = h_ref[...]
    u = _dot(h, wu_ref[...])
    g = _dot(h, wg_ref[...])
    ws = ws_ref[0]
    bs = bs_ref[0]
    for c in range(tm // chunk):
        sl = slice(c * chunk, (c + 1) * chunk)
        vs = _dot(ws, vn_ref[sl, :]) + bs
        y_ref[sl, :] = (u[sl] * vs * _silu(g[sl])).astype(BF16)


def _spatial_gate(h, w_in, vn, w_s, b_s, *, tm):
    n_tok, d = h.shape
    groups, chunk, _ = w_s.shape
    sgw = vn.shape[1]
    tn = sgw // groups
    return pl.pallas_call(
        functools.partial(_sg_kernel, tm=tm, chunk=chunk),
        out_shape=jax.ShapeDtypeStruct((n_tok, sgw), BF16),
        grid=(n_tok // tm, groups),
        in_specs=[
            pl.BlockSpec((tm, d), lambda i, j: (i, 0)),
            pl.BlockSpec((d, tn), lambda i, j: (0, j)),
            pl.BlockSpec((d, tn), lambda i, j: (0, 2 * groups + j)),
            pl.BlockSpec((tm, tn), lambda i, j: (i, j)),
            pl.BlockSpec((1, chunk, chunk), lambda i, j: (j, 0, 0)),
            pl.BlockSpec((1, chunk, 1), lambda i, j: (j, 0, 0)),
        ],
        out_specs=pl.BlockSpec((tm, tn), lambda i, j: (i, j)),
        compiler_params=_params("parallel", "arbitrary"),
        name="odd_spatial_gate",
    )(h, w_in, w_in, vn, w_s, b_s)


def _rot_cols(w):
    shp = w.shape
    w4 = w.reshape(shp[:-1] + (2, 2, QK_ROPE // 4))
    return jnp.stack([-w4[..., 1, :], w4[..., 0, :]], axis=-2).reshape(shp)


def _rope_table(n):
    rows = n // GRID_W
    row = jnp.repeat(jnp.arange(rows, dtype=F32), GRID_W)
    col = jnp.tile(jnp.arange(GRID_W, dtype=F32), rows)
    n_freq = QK_ROPE // 4
    inv = ROPE_BASE ** (-jnp.arange(n_freq, dtype=F32) / n_freq)
    ar = row[:, None] * inv
    ac = col[:, None] * inv
    ang = jnp.concatenate([ar, ar, ac, ac], axis=-1)
    return jnp.concatenate([jnp.cos(ang), jnp.sin(ang)], axis=-1)


def _tile(n, pref):
    return pref if n % pref == 0 else n


def _prep_even(e_w_in, e_w_qb, e_w_kvb, e_w_out, i, cw, ql, kvl, heads):
    w_in = e_w_in[i]
    o = 4 * cw
    w_kpe = w_in[:, o + ql + kvl:o + ql + kvl + QK_ROPE]
    wq = e_w_qb[i].reshape(ql, heads, QK_NOPE + QK_ROPE)
    pe = wq[..., QK_NOPE:]
    w_qb_ext = jnp.concatenate([wq[..., :QK_NOPE], pe, _rot_cols(pe)], axis=-1).reshape(ql, heads * HEAD_QK)
    return dict(
        w_conv=w_in[:, :o].astype(BF16),
        w_qa=w_in[:, o:o + ql].astype(BF16),
        w_ckv=w_in[:, o + ql:o + ql + kvl].astype(BF16),
        w_kpe2=jnp.concatenate([w_kpe, _rot_cols(w_kpe)], axis=-1).astype(BF16),
        w_mg=w_in[:, o + ql + kvl + QK_ROPE:].astype(BF16),
        w_qb_ext=w_qb_ext.astype(BF16),
        w_kvb=e_w_kvb[i].astype(BF16),
        w_out1=e_w_out[i][:cw].astype(BF16),
        w_out2=e_w_out[i][cw:].astype(BF16),
    )


def _prep_odd(o_w_in, o_w_s, o_w_out, i):
    half = o_w_out.shape[1] // 2
    return dict(w_in=o_w_in[i].astype(BF16), w_s=o_w_s[i].astype(BF16),
                w_out1=o_w_out[i][:half].astype(BF16), w_out2=o_w_out[i][half:].astype(BF16))


def _trunk(x, *, batch, seq, cond_row, cs, cache, mod, norm_g, even, e_conv_w, e_q_norm_g, e_kv_norm_g,
           odd, o_ln_g, o_ln_b, o_b_s, final_g, heads, emit):
    n_tok, d = x.shape
    depth = mod.shape[0]
    assert seq & (seq - 1) == 0, "sequence length must be a power of two"
    tm = _tile(n_tok, 512)
    tml = _tile(n_tok, 1024)
    scale = float((QK_NOPE + QK_ROPE) ** -0.5 * math.log2(math.e))
    row_of = cond_row
    mods = [mod[l].reshape(mod.shape[1], 1, mod.shape[2]) for l in range(depth)]
    ckvs, kpes = [], []
    h = _first_h(x, mods[0], norm_g[0][None, :], tm=tm, row_of=row_of)
    for l in range(depth):
        i = l // 2
        if l % 2 == 0:
            w = even[i]
            assert w["w_out1"].shape[0] == w["w_out2"].shape[0] == heads * V_HEAD
            y1 = _conv_branch(h, w["w_conv"], e_conv_w[i], tm=tml, tn=_tile(e_conv_w.shape[2], 256), seq=seq)
            q = _q_proj(h, w["w_qa"], e_q_norm_g[i][None, :], w["w_qb_ext"], cs, tm=tm, seq=seq,
                        heads=heads, scale=scale)
            kv = _kv_proj(h, w["w_ckv"], e_kv_norm_g[i][None, :], w["w_kpe2"], w["w_kvb"], cs, tm=tm, seq=seq,
                          heads=heads, emit=emit)
            if emit:
                ckvs.append(kv[2])
                kpes.append(kv[3])
            gate = _gate(h, w["w_mg"], tm=tm, tn=_tile(w["w_mg"].shape[1], 512))
            if cache is None:
                y2 = _attention(q, kv[0], kv[1], gate, batch=batch, seq=seq, heads=heads, tq=_tile(seq, 256))
            else:
                c_ckv, c_kpe = cache
                past = c_ckv.shape[2]
                kvl = c_ckv.shape[3]
                kpe_pad = jnp.pad(c_kpe[:, i].reshape(batch * past, QK_ROPE),
                                  ((0, 0), (0, LANES - QK_ROPE))).astype(BF16)
                kc, vc = _cache_kv(c_ckv[:, i].reshape(batch * past, kvl), kpe_pad, w["w_kvb"],
                                   tm=_tile(batch * past, 512), heads=heads)
                y2 = _attention_pipelined(q, kc, vc, kv[0], kv[1], gate, batch=batch, seq=seq, past=past,
                                          heads=heads, tq=_tile(seq, 512), kc_size=_tile(math.gcd(past, seq), 512))
            c2 = 0
        else:
            w = odd[i]
            vn = _v_ln(h, w["w_in"], o_ln_g[i][None, :], o_ln_b[i][None, :], tm=tm,
                       tnv=_tile(o_ln_g.shape[1], 512))
            y1 = y2 = _spatial_gate(h, w["w_in"], vn, w["w_s"], o_b_s[i][:, :, None], tm=tml)
            c2 = 1
        if l + 1 < depth:
            x, h = _out_proj(y1, y2, 0, c2, w["w_out1"], w["w_out2"], x, mods[l], norm_g[l + 1][None, :],
                             mods[l + 1], tm=tm, tn=_tile(d, 512), row_of=row_of)
        else:
            x = _out_proj(y1, y2, 0, c2, w["w_out1"], w["w_out2"], x, mods[l], final_g[None, :], None,
                          tm=tm, tn=_tile(d, 512), row_of=row_of)
    return x, ckvs, kpes


def kernel(x_prompt, x_sample, cache_ckv, cache_kpe, c, c_ctx, norm_g, w_ada, b_ada, e_w_in, e_conv_w,
           e_q_norm_g, e_w_qb, e_kv_norm_g, e_w_kvb, e_w_out, o_w_in, o_ln_g, o_ln_b, o_w_s, o_b_s, o_w_out,
           final_g):
    batch, seq, d = x_prompt.shape
    dec_batch, dec_seq, _ = x_sample.shape
    cw = e_conv_w.shape[2]
    ql = e_q_norm_g.shape[1]
    kvl = e_kv_norm_g.shape[1]
    heads = e_w_qb.shape[2] // (QK_NOPE + QK_ROPE)
    n_even, n_odd = e_w_in.shape[0], o_w_in.shape[0]

    rows = -(-(dec_batch + 1) // SUBLANES) * SUBLANES
    cond = jnp.concatenate([c, c_ctx[None, :], jnp.zeros((rows - dec_batch - 1, d), F32)], axis=0)
    mod = _ada(cond, w_ada, b_ada)

    even = [_prep_even(e_w_in, e_w_qb, e_w_kvb, e_w_out, i, cw, ql, kvl, heads) for i in range(n_even)]
    odd = [_prep_odd(o_w_in, o_w_s, o_w_out, i) for i in range(n_odd)]
    shared = dict(mod=mod, norm_g=norm_g, even=even, e_conv_w=e_conv_w, e_q_norm_g=e_q_norm_g,
                  e_kv_norm_g=e_kv_norm_g, odd=odd, o_ln_g=o_ln_g, o_ln_b=o_ln_b, o_b_s=o_b_s,
                  final_g=final_g, heads=heads)

    y_prompt, ckvs, kpes = _trunk(x_prompt.reshape(batch * seq, d), batch=batch, seq=seq,
                                  cond_row=lambda r: dec_batch, cs=None, cache=None, emit=True, **shared)
    y_sample, _, _ = _trunk(x_sample.reshape(dec_batch * dec_seq, d), batch=dec_batch, seq=dec_seq,
                            cond_row=lambda r: r // dec_seq, cs=_rope_table(dec_seq),
                            cache=(cache_ckv, cache_kpe), emit=False, **shared)
    new_ckv = jnp.stack([t.reshape(batch, seq, kvl) for t in ckvs], axis=1)
    new_kpe = jnp.stack([t.reshape(batch, seq, QK_ROPE) for t in kpes], axis=1)
    return (y_prompt.reshape(batch, seq, d), y_sample.reshape(dec_batch, dec_seq, d), new_ckv, new_kpe)
```

```python
import functools
import math

import jax
import jax.numpy as jnp
import numpy as np
from jax import lax
from jax.experimental import pallas as pl
from jax.experimental.pallas import tpu as pltpu

F32 = jnp.float32
BF16 = jnp.bfloat16

EPS = 1e-6
QK_NOPE = 128
QK_ROPE = 64
V_HEAD = 128
HEAD_QK = 256
HEAD_V = 256
GRID_W = 64
ROPE_BASE = 10000.0

V7X_VMEM_BYTES = 64 * 1024 * 1024
VMEM_LIMIT = V7X_VMEM_BYTES - 8 * 1024 * 1024
SUBLANES = 8
PACKED_ROWS = 16
LANES = 128


def _silu(x):
    return x * jax.nn.sigmoid(x)


def _dot(a, b):
    return jnp.dot(a, b, preferred_element_type=F32)


def _dot_nt(a, b):
    return lax.dot_general(a, b, (((1,), (1,)), ((), ())), preferred_element_type=F32)


def _rms(x, g):
    return x * lax.rsqrt(jnp.mean(x * x, axis=-1, keepdims=True) + EPS) * g


def _params(*sem):
    return pltpu.CompilerParams(dimension_semantics=sem, vmem_limit_bytes=VMEM_LIMIT)


def _ada_kernel(c_ref, w_ref, b_ref, o_ref):
    a = _silu(c_ref[...]).astype(BF16)
    o_ref[0] = _dot(a, w_ref[0].astype(BF16)) + b_ref[0]


def _ada(cond, w_ada, b_ada):
    depth, d, n = w_ada.shape
    rows = cond.shape[0]
    tn = 768 if n % 768 == 0 else n
    return pl.pallas_call(
        _ada_kernel,
        out_shape=jax.ShapeDtypeStruct((depth, rows, n), F32),
        grid=(depth, n // tn),
        in_specs=[
            pl.BlockSpec((rows, d), lambda l, j: (0, 0)),
            pl.BlockSpec((1, d, tn), lambda l, j: (l, 0, j)),
            pl.BlockSpec((1, 1, tn), lambda l, j: (l, 0, j)),
        ],
        out_specs=pl.BlockSpec((1, rows, tn), lambda l, j: (l, 0, j)),
        compiler_params=_params("parallel", "parallel"),
        name="ada_mod",
    )(cond, w_ada, b_ada.reshape(depth, 1, n))


def _norm_mod(x, g, sc, sh):
    return _rms(x, g) * (1.0 + sc) + sh


def _norm_mod_kernel(x_ref, ng_ref, sh_ref, sc_ref, h_ref):
    h_ref[...] = _norm_mod(x_ref[...], ng_ref[...], sc_ref[0], sh_ref[0]).astype(BF16)


def _first_h(x, mod_l, norm_g, *, tm, row_of):
    n_tok, d = x.shape
    return pl.pallas_call(
        _norm_mod_kernel,
        out_shape=jax.ShapeDtypeStruct((n_tok, d), BF16),
        grid=(n_tok // tm,),
        in_specs=[
            pl.BlockSpec((tm, d), lambda i: (i, 0)),
            pl.BlockSpec((1, d), lambda i: (0, 0)),
            pl.BlockSpec((1, 1, d), lambda i: (row_of(i * tm), 0, 0)),
            pl.BlockSpec((1, 1, d), lambda i: (row_of(i * tm), 0, 1)),
        ],
        out_specs=pl.BlockSpec((tm, d), lambda i: (i, 0)),
        compiler_params=_params("parallel"),
        name="norm_mod",
    )(x, norm_g, mod_l, mod_l)


def _conv_kernel(h_ref, hp_ref, hn_ref, wb_ref, wc_ref, wx_ref, wg_ref, cw_ref, y_ref, hx_scr, *, tm, seq):
    i = pl.program_id(0)
    j = pl.program_id(1)

    @pl.when(j == 0)
    def _():
        hx_scr[0:tm] = h_ref[...]
        hx_scr[tm:tm + PACKED_ROWS] = hp_ref[...]
        hx_scr[tm + PACKED_ROWS:tm + 2 * PACKED_ROWS] = hn_ref[...]

    he = hx_scr[...]
    pe = _dot(he, wc_ref[...]) * _dot(he, wx_ref[...])
    p = pe[:tm]
    p_before = pe[tm + PACKED_ROWS - 1:tm + PACKED_ROWS]
    p_after = pe[tm + PACKED_ROWS:tm + PACKED_ROWS + 1]
    hm = h_ref[...]
    cb = _dot(hm, wb_ref[...])
    cg = _dot(hm, wg_ref[...])

    row = lax.broadcasted_iota(jnp.int32, p.shape, 0)
    pos = (i * tm + row) & (seq - 1)
    prev = jnp.where(row == 0, p_before, pltpu.roll(p, 1, 0))
    prev = jnp.where(pos == 0, 0.0, prev)
    nxt = jnp.where(row == tm - 1, p_after, pltpu.roll(p, tm - 1, 0))
    nxt = jnp.where(pos == seq - 1, 0.0, nxt)
    cw = cw_ref[...]
    conv = prev * cw[0:1] + p * cw[1:2] + nxt * cw[2:3]
    y_ref[...] = (cb * conv * _silu(cg)).astype(BF16)


def _conv_branch(h, w_in, li, conv_w, *, tm, tn, seq):
    n_tok, d = h.shape
    cw = conv_w.shape[1]
    nj = cw // tn
    n16 = n_tok // PACKED_ROWS
    t16 = tm // PACKED_ROWS
    wspec = lambda grp: pl.BlockSpec((None, d, tn), lambda i, j: (li, 0, grp * nj + j))
    return pl.pallas_call(
        functools.partial(_conv_kernel, tm=tm, seq=seq),
        out_shape=jax.ShapeDtypeStruct((n_tok, cw), BF16),
        grid=(n_tok // tm, nj),
        in_specs=[
            pl.BlockSpec((tm, d), lambda i, j: (i, 0)),
            pl.BlockSpec((PACKED_ROWS, d), lambda i, j: (jnp.maximum(i * t16 - 1, 0), 0)),
            pl.BlockSpec((PACKED_ROWS, d), lambda i, j: (jnp.minimum((i + 1) * t16, n16 - 1), 0)),
            wspec(0), wspec(1), wspec(2), wspec(3),
            pl.BlockSpec((3, tn), lambda i, j: (0, j)),
        ],
        out_specs=pl.BlockSpec((tm, tn), lambda i, j: (i, j)),
        scratch_shapes=[pltpu.VMEM((tm + 2 * PACKED_ROWS, d), BF16)],
        compiler_params=_params("parallel", "arbitrary"),
        name="even_conv",
    )(h, h, h, w_in, w_in, w_in, w_in, conv_w)


def _rope_pair(v, cs):
    t = v * cs
    return t + pltpu.roll(t, QK_ROPE, 1)


def _q_kernel(*refs, heads, rope, scale):
    if rope:
        h_ref, wqa_ref, qg_ref, wqb_ref, cs_ref, q_ref = refs
    else:
        h_ref, wqa_ref, qg_ref, wqb_ref, q_ref = refs
    qa = _dot(h_ref[...], wqa_ref[...])
    qn = _rms(qa, qg_ref[...]).astype(BF16)
    for hd in range(heads):
        base = hd * HEAD_QK
        q = _dot(qn, wqb_ref[:, base:base + HEAD_QK])
        pe = q[:, QK_NOPE:]
        if rope:
            pe = _rope_pair(pe, cs_ref[...])
        q_ref[:, base:base + QK_NOPE] = (q[:, :QK_NOPE] * scale).astype(BF16)
        q_ref[:, base + QK_NOPE:base + HEAD_QK] = (pe * scale).astype(BF16)


def _q_proj(h, w_in, li, qa_block, q_norm_g, w_qb_ext, cs, *, tm, seq, heads, scale):
    n_tok, d = h.shape
    ql = q_norm_g.shape[1]
    rope = cs is not None
    in_specs = [
        pl.BlockSpec((tm, d), lambda i: (i, 0)),
        pl.BlockSpec((None, d, ql), lambda i: (li, 0, qa_block), pipeline_mode=pl.Buffered(1)),
        pl.BlockSpec((1, ql), lambda i: (0, 0)),
        pl.BlockSpec((None, ql, heads * HEAD_QK), lambda i: (li, 0, 0), pipeline_mode=pl.Buffered(1)),
    ]
    args = [h, w_in, q_norm_g, w_qb_ext]
    if rope:
        nb = seq // tm
        in_specs.append(pl.BlockSpec((tm, LANES), lambda i: (i % nb, 0)))
        args.append(cs)
    return pl.pallas_call(
        functools.partial(_q_kernel, heads=heads, rope=rope, scale=scale),
        out_shape=jax.ShapeDtypeStruct((n_tok, heads * HEAD_QK), BF16),
        grid=(n_tok // tm,),
        in_specs=in_specs,
        out_specs=pl.BlockSpec((tm, heads * HEAD_QK), lambda i: (i, 0)),
        compiler_params=_params("parallel"),
        name="mla_q_proj",
    )(*args)


def _write_kv(cb, kper, wkvb_ref, k_ref, v_ref, heads):
    ones = jnp.ones((cb.shape[0], HEAD_V - V_HEAD), BF16)
    for hd in range(heads):
        base = hd * HEAD_QK
        kv = _dot(cb, wkvb_ref[:, base:base + QK_NOPE + V_HEAD])
        k_ref[:, base:base + QK_NOPE] = kv[:, :QK_NOPE].astype(BF16)
        k_ref[:, base + QK_NOPE:base + HEAD_QK] = kper
        v_ref[:, hd * HEAD_V:hd * HEAD_V + V_HEAD] = kv[:, QK_NOPE:].astype(BF16)
        v_ref[:, hd * HEAD_V + V_HEAD:(hd + 1) * HEAD_V] = ones


def _kv_kernel(*refs, heads, rope, emit):
    refs = list(refs)
    h_ref, wckv_ref, kg_ref, wkpe_ref, wkvb_ref = refs[:5]
    rest = refs[5:]
    cs_ref = rest.pop(0) if rope else None
    k_ref, v_ref = rest[:2]
    h = h_ref[...]
    ckv = _rms(_dot(h, wckv_ref[...]), kg_ref[...])
    kp = _dot(h, wkpe_ref[...])
    if emit:
        ckv_ref, kpe_ref = rest[2:]
        ckv_ref[...] = ckv
        kpe_ref[...] = kp[:, :QK_ROPE]
    kk = _rope_pair(kp, cs_ref[...]) if rope else kp
    lane = lax.broadcasted_iota(jnp.int32, kk.shape, 1)
    kper = jnp.where(lane < QK_ROPE, kk, 0.0).astype(BF16)
    _write_kv(ckv.astype(BF16), kper, wkvb_ref, k_ref, v_ref, heads)


def _kv_proj(h, w_in, li, ckv_block, kv_norm_g, w_kpe2, w_kvb, cs, *, tm, seq, heads, emit):
    n_tok, d = h.shape
    kvl = kv_norm_g.shape[1]
    rope = cs is not None
    in_specs = [
        pl.BlockSpec((tm, d), lambda i: (i, 0)),
        pl.BlockSpec((None, d, kvl), lambda i: (li, 0, ckv_block), pipeline_mode=pl.Buffered(1)),
        pl.BlockSpec((1, kvl), lambda i: (0, 0)),
        pl.BlockSpec((d, LANES), lambda i: (0, 0)),
        pl.BlockSpec((None, kvl, heads * HEAD_QK), lambda i: (li, 0, 0), pipeline_mode=pl.Buffered(1)),
    ]
    args = [h, w_in, kv_norm_g, w_kpe2, w_kvb]
    if rope:
        nb = seq // tm
        in_specs.append(pl.BlockSpec((tm, LANES), lambda i: (i % nb, 0)))
        args.append(cs)
    out_shape = [jax.ShapeDtypeStruct((n_tok, heads * HEAD_QK), BF16),
                 jax.ShapeDtypeStruct((n_tok, heads * HEAD_V), BF16)]
    out_specs = [pl.BlockSpec((tm, heads * HEAD_QK), lambda i: (i, 0)),
                 pl.BlockSpec((tm, heads * HEAD_V), lambda i: (i, 0))]
    if emit:
        out_shape += [jax.ShapeDtypeStruct((n_tok, kvl), F32), jax.ShapeDtypeStruct((n_tok, QK_ROPE), F32)]
        out_specs += [pl.BlockSpec((tm, kvl), lambda i: (i, 0)), pl.BlockSpec((tm, QK_ROPE), lambda i: (i, 0))]
    return pl.pallas_call(
        functools.partial(_kv_kernel, heads=heads, rope=rope, emit=emit),
        out_shape=tuple(out_shape),
        grid=(n_tok // tm,),
        in_specs=in_specs,
        out_specs=tuple(out_specs),
        compiler_params=_params("parallel"),
        name="mla_kv_proj",
    )(*args)


def _cache_kv_kernel(c_ref, kpe_ref, wkvb_ref, k_ref, v_ref, *, heads):
    _write_kv(c_ref[...].astype(BF16), kpe_ref[...], wkvb_ref, k_ref, v_ref, heads)


def _cache_kv(ckv, kpe_pad, w_kvb, li, *, tm, heads):
    n, kvl = ckv.shape
    return pl.pallas_call(
        functools.partial(_cache_kv_kernel, heads=heads),
        out_shape=(jax.ShapeDtypeStruct((n, heads * HEAD_QK), BF16),
                   jax.ShapeDtypeStruct((n, heads * HEAD_V), BF16)),
        grid=(n // tm,),
        in_specs=[
            pl.BlockSpec((tm, kvl), lambda i: (i, 0)),
            pl.BlockSpec((tm, LANES), lambda i: (i, 0)),
            pl.BlockSpec((None, kvl, heads * HEAD_QK), lambda i: (li, 0, 0), pipeline_mode=pl.Buffered(1)),
        ],
        out_specs=(pl.BlockSpec((tm, heads * HEAD_QK), lambda i: (i, 0)),
                   pl.BlockSpec((tm, heads * HEAD_V), lambda i: (i, 0))),
        compiler_params=_params("parallel"),
        name="mla_cache_kv",
    )(ckv, kpe_pad, w_kvb)


def _gate_kernel(h_ref, w_ref, o_ref, *, tn):
    h = h_ref[...]
    for c in range(o_ref.shape[1] // tn):
        sl = slice(c * tn, (c + 1) * tn)
        o_ref[:, sl] = _silu(_dot(h, w_ref[:, sl])).astype(BF16)


def _gate(h, w, *, tm, tn):
    n_tok, d = h.shape
    n = w.shape[1]
    return pl.pallas_call(
        functools.partial(_gate_kernel, tn=tn),
        out_shape=jax.ShapeDtypeStruct((n_tok, n), BF16),
        grid=(n_tok // tm,),
        in_specs=[pl.BlockSpec((tm, d), lambda i: (i, 0)), pl.BlockSpec((d, n), lambda i: (0, 0))],
        out_specs=pl.BlockSpec((tm, n), lambda i: (i, 0)),
        compiler_params=_params("parallel"),
        name="mla_gate",
    )(h, w)


def _finish(acc, gate, o_ref, rows=slice(None)):
    o_ref[rows, :] = (acc[:, :V_HEAD] / acc[:, V_HEAD:] * gate.astype(F32)).astype(BF16)


def _attn_kernel(q_ref, k_ref, v_ref, gate_ref, o_ref, *, heads):
    for hd in range(heads):
        qk = slice(hd * HEAD_QK, (hd + 1) * HEAD_QK)
        s = _dot_nt(q_ref[:, qk], k_ref[:, qk])
        p = jnp.exp2(s - jnp.max(s, axis=-1, keepdims=True))
        acc = _dot(p.astype(BF16), v_ref[:, hd * HEAD_V:(hd + 1) * HEAD_V])
        cols = slice(hd * V_HEAD, (hd + 1) * V_HEAD)
        o_ref[:, cols] = (acc[:, :V_HEAD] / acc[:, V_HEAD:] * gate_ref[:, cols].astype(F32)).astype(BF16)


def _attention(q, k, v, gate, *, batch, seq, heads):
    n_tok = q.shape[0]
    return pl.pallas_call(
        functools.partial(_attn_kernel, heads=heads),
        out_shape=jax.ShapeDtypeStruct((n_tok, heads * V_HEAD), BF16),
        grid=(batch,),
        in_specs=[
            pl.BlockSpec((seq, heads * HEAD_QK), lambda b: (b, 0)),
            pl.BlockSpec((seq, heads * HEAD_QK), lambda b: (b, 0)),
            pl.BlockSpec((seq, heads * HEAD_V), lambda b: (b, 0)),
            pl.BlockSpec((seq, heads * V_HEAD), lambda b: (b, 0)),
        ],
        out_specs=pl.BlockSpec((seq, heads * V_HEAD), lambda b: (b, 0)),
        compiler_params=_params("parallel"),
        name="mla_attention",
    )(q, k, v, gate)


def _attn_pipe_kernel(q_ref, kc_ref, ko_ref, vc_ref, vo_ref, gate_ref, o_ref, s_scr, m_scr, *, kc, nsub):
    @pl.when(jnp.logical_and(pl.program_id(0) == 0, pl.program_id(1) == 0))
    def _():
        s_scr[...] = jnp.zeros_like(s_scr)
        m_scr[...] = jnp.zeros_like(m_scr)

    tq = q_ref.shape[0] // nsub
    for u in range(nsub):
        qrows = slice(u * tq, (u + 1) * tq)
        q = q_ref[qrows, :]
        m_old = m_scr[u]
        acc = None
        m_acc = None
        off = 0
        for k_ref, v_ref in ((kc_ref, vc_ref), (ko_ref, vo_ref)):
            for c in range(k_ref.shape[0] // kc):
                rows = slice(c * kc, (c + 1) * kc)
                cols = slice(off, off + kc)
                s_old = s_scr[u, :, cols]
                p = jnp.concatenate(
                    [jnp.exp2(s_old[:, t * LANES:(t + 1) * LANES] - m_old) for t in range(kc // LANES)], axis=1)
                pv = _dot(p.astype(BF16), v_ref[rows, :])
                acc = pv if acc is None else acc + pv
                s_new = _dot_nt(q, k_ref[rows, :])
                s_scr[u, :, cols] = s_new
                for t in range(kc // LANES):
                    blk = s_new[:, t * LANES:(t + 1) * LANES]
                    m_acc = blk if m_acc is None else jnp.maximum(m_acc, blk)
                off += kc
        m_scr[u] = jnp.broadcast_to(jnp.max(m_acc, axis=-1, keepdims=True), (tq, LANES))
        _finish(acc, gate_ref[qrows, :], o_ref, qrows)


def _attention_pipelined(q, kc, vc, ko, vo, gate, *, batch, seq, past, heads, tq, nsub, kc_size):
    n_tok = q.shape[0]
    tsub, tq = tq, tq * nsub
    nq = seq // tq
    jobs = heads * nq
    jq = lambda g: jnp.minimum(g, jobs - 1)
    jp = lambda g: jnp.maximum(g - 1, 0)
    return pl.pallas_call(
        functools.partial(_attn_pipe_kernel, kc=kc_size, nsub=nsub),
        out_shape=jax.ShapeDtypeStruct((n_tok, heads * V_HEAD), BF16),
        grid=(batch, jobs + 1),
        in_specs=[
            pl.BlockSpec((tq, HEAD_QK), lambda b, g: (b * nq + jq(g) % nq, jq(g) // nq)),
            pl.BlockSpec((past, HEAD_QK), lambda b, g: (b, jq(g) // nq)),
            pl.BlockSpec((seq, HEAD_QK), lambda b, g: (b, jq(g) // nq)),
            pl.BlockSpec((past, HEAD_V), lambda b, g: (b, jp(g) // nq)),
            pl.BlockSpec((seq, HEAD_V), lambda b, g: (b, jp(g) // nq)),
            pl.BlockSpec((tq, V_HEAD), lambda b, g: (b * nq + jp(g) % nq, jp(g) // nq)),
        ],
        out_specs=pl.BlockSpec((tq, V_HEAD), lambda b, g: (b * nq + jp(g) % nq, jp(g) // nq)),
        scratch_shapes=[pltpu.VMEM((nsub, tsub, past + seq), F32), pltpu.VMEM((nsub, tsub, LANES), F32)],
        compiler_params=_params("arbitrary", "arbitrary"),
        name="mla_attention_pipelined",
    )(q, kc, ko, vc, vo, gate)


def _out_kernel(*refs, tn, last):
    if last:
        y1_ref, y2_ref, w1_ref, w2_ref, x_ref, g_ref, ng_ref, o_ref = refs
    else:
        y1_ref, y2_ref, w1_ref, w2_ref, x_ref, g_ref, ng_ref, sh_ref, sc_ref, o_ref, h_ref = refs
    d = x_ref.shape[1]
    y1 = y1_ref[...]
    y2 = y2_ref[...]
    g = g_ref[0]
    ss = None
    for c in range(d // tn):
        sl = slice(c * tn, (c + 1) * tn)
        acc = _dot(y1, w1_ref[:, sl]) + _dot(y2, w2_ref[:, sl])
        xn = x_ref[:, sl] + g[:, sl] * acc
        o_ref[:, sl] = xn
        part = jnp.sum(xn * xn, axis=-1, keepdims=True)
        ss = part if ss is None else ss + part
    inv = lax.rsqrt(ss / d + EPS)
    for c in range(d // tn):
        sl = slice(c * tn, (c + 1) * tn)
        normed = o_ref[:, sl] * inv * ng_ref[:, sl]
        if last:
            o_ref[:, sl] = normed
        else:
            h_ref[:, sl] = (normed * (1.0 + sc_ref[0][:, sl]) + sh_ref[0][:, sl]).astype(BF16)


def _out_proj(y1, y2, c1, c2, w_out, li, x, mod_l, next_g, mod_next, *, tm, tn, row_of):
    n_tok, d = x.shape
    kh = w_out.shape[1] // 2
    last = mod_next is None
    in_specs = [
        pl.BlockSpec((tm, kh), lambda i: (i, c1)),
        pl.BlockSpec((tm, kh), lambda i: (i, c2)),
        pl.BlockSpec((None, kh, d), lambda i: (li, 0, 0), pipeline_mode=pl.Buffered(1)),
        pl.BlockSpec((None, kh, d), lambda i: (li, 1, 0), pipeline_mode=pl.Buffered(1)),
        pl.BlockSpec((tm, d), lambda i: (i, 0)),
        pl.BlockSpec((1, 1, d), lambda i: (row_of(i * tm), 0, 2)),
        pl.BlockSpec((1, d), lambda i: (0, 0)),
    ]
    args = [y1, y2, w_out, w_out, x, mod_l, next_g]
    out_shape = [jax.ShapeDtypeStruct((n_tok, d), F32)]
    out_specs = [pl.BlockSpec((tm, d), lambda i: (i, 0))]
    if not last:
        in_specs += [pl.BlockSpec((1, 1, d), lambda i: (row_of(i * tm), 0, 0)),
                     pl.BlockSpec((1, 1, d), lambda i: (row_of(i * tm), 0, 1))]
        args += [mod_next, mod_next]
        out_shape.append(jax.ShapeDtypeStruct((n_tok, d), BF16))
        out_specs.append(pl.BlockSpec((tm, d), lambda i: (i, 0)))
    res = pl.pallas_call(
        functools.partial(_out_kernel, tn=tn, last=last),
        out_shape=tuple(out_shape),
        grid=(n_tok // tm,),
        in_specs=in_specs,
        out_specs=tuple(out_specs),
        compiler_params=_params("parallel"),
        name="out_proj",
    )(*args)
    return res[0] if last else res


def _v_ln_kernel(h_ref, wv_ref, lg_ref, lb_ref, vn_ref, vacc, mu_scr, *, nj, tnv):
    @pl.when(pl.program_id(0) == 0)
    def _():
        vacc[...] = jnp.zeros_like(vacc)
        mu_scr[...] = jnp.zeros_like(mu_scr)

    width = nj * tnv
    mu = mu_scr[:, 0:1]
    sq = functools.reduce(
        jnp.add, [jnp.sum((vacc[t] - mu) * (vacc[t] - mu), axis=-1, keepdims=True) for t in range(nj)])
    inv = lax.rsqrt(sq / width + EPS)
    h = h_ref[...]
    tot = None
    for t in range(nj):
        sl = slice(t * tnv, (t + 1) * tnv)
        vn_ref[:, sl] = ((vacc[t] - mu) * inv * lg_ref[:, sl] + lb_ref[:, sl]).astype(BF16)
        v_new = _dot(h, wv_ref[:, sl])
        vacc[t] = v_new
        part = jnp.sum(v_new, axis=-1, keepdims=True)
        tot = part if tot is None else tot + part
    mu_scr[...] = jnp.broadcast_to(tot / width, mu_scr.shape)


def _v_ln(h, w_in, li, ln_g, ln_b, *, tm, tnv):
    n_tok, d = h.shape
    sgw = ln_g.shape[1]
    nj = sgw // tnv
    n = n_tok // tm
    return pl.pallas_call(
        functools.partial(_v_ln_kernel, nj=nj, tnv=tnv),
        out_shape=jax.ShapeDtypeStruct((n_tok, sgw), BF16),
        grid=(n + 1,),
        in_specs=[
            pl.BlockSpec((tm, d), lambda i: (jnp.minimum(i, n - 1), 0)),
            pl.BlockSpec((None, d, sgw), lambda i: (li, 0, 1), pipeline_mode=pl.Buffered(1)),
            pl.BlockSpec((1, sgw), lambda i: (0, 0)),
            pl.BlockSpec((1, sgw), lambda i: (0, 0)),
        ],
        out_specs=pl.BlockSpec((tm, sgw), lambda i: (jnp.maximum(i - 1, 0), 0)),
        scratch_shapes=[pltpu.VMEM((nj, tm, tnv), F32), pltpu.VMEM((tm, LANES), F32)],
        compiler_params=_params("arbitrary"),
        name="odd_v_ln",
    )(h, w_in, ln_g, ln_b)


def _sg_kernel(h_ref, wu_ref, wg_ref, vn_ref, ws_ref, bs_ref, y_ref, *, tm, chunk):
    h = h_ref[...]
    u = _dot(h, wu_ref[...])
    g = _dot(h, wg_ref[...])
    ws = ws_ref[...]
    bs = bs_ref[...]
    for c in range(tm // chunk):
        sl = slice(c * chunk, (c + 1) * chunk)
        vs = _dot(ws, vn_ref[sl, :]) + bs
        y_ref[sl, :] = (u[sl] * vs * _silu(g[sl])).astype(BF16)


def _spatial_gate(h, w_in, li, vn, w_s, b_s, *, tm):
    n_tok, d = h.shape
    _, groups, chunk, _ = w_s.shape
    sgw = vn.shape[1]
    tn = sgw // groups
    return pl.pallas_call(
        functools.partial(_sg_kernel, tm=tm, chunk=chunk),
        out_shape=jax.ShapeDtypeStruct((n_tok, sgw), BF16),
        grid=(n_tok // tm, groups),
        in_specs=[
            pl.BlockSpec((tm, d), lambda i, j: (i, 0)),
            pl.BlockSpec((None, d, tn), lambda i, j: (li, 0, j)),
            pl.BlockSpec((None, d, tn), lambda i, j: (li, 0, 2 * groups + j)),
            pl.BlockSpec((tm, tn), lambda i, j: (i, j)),
            pl.BlockSpec((None, None, chunk, chunk), lambda i, j: (li, j, 0, 0)),
            pl.BlockSpec((None, None, chunk, 1), lambda i, j: (li, j, 0, 0)),
        ],
        out_specs=pl.BlockSpec((tm, tn), lambda i, j: (i, j)),
        compiler_params=_params("parallel", "arbitrary"),
        name="odd_spatial_gate",
    )(h, w_in, w_in, vn, w_s, b_s)


def _rot_cols(w):
    shp = w.shape
    w4 = w.reshape(shp[:-1] + (2, 2, QK_ROPE // 4))
    return jnp.stack([-w4[..., 1, :], w4[..., 0, :]], axis=-2).reshape(shp)


def _rope_table(n):
    f32 = np.float32
    rows = n // GRID_W
    row = np.repeat(np.arange(rows, dtype=f32), GRID_W)
    col = np.tile(np.arange(GRID_W, dtype=f32), rows)
    n_freq = QK_ROPE // 4
    inv = np.power(f32(ROPE_BASE), -np.arange(n_freq, dtype=f32) / f32(n_freq)).astype(f32)
    ar = row[:, None] * inv
    ac = col[:, None] * inv
    ang = np.concatenate([ar, ar, ac, ac], axis=-1).astype(f32)
    return jnp.asarray(np.concatenate([np.cos(ang), np.sin(ang)], axis=-1).astype(f32))


def _tile(n, pref):
    return pref if n % pref == 0 else n


def _prep_weights(e_w_in, e_w_qb, e_w_kvb, e_w_out, o_w_in, o_w_s, o_w_out, cw, ql, kvl, heads):
    n_even = e_w_in.shape[0]
    e_in = e_w_in.astype(BF16)
    o = 4 * cw + ql + kvl
    w_kpe = e_in[:, :, o:o + QK_ROPE]
    wq = e_w_qb.astype(BF16).reshape(n_even, ql, heads, QK_NOPE + QK_ROPE)
    pe = wq[..., QK_NOPE:]
    w_qb_ext = jnp.concatenate([wq[..., :QK_NOPE], pe, _rot_cols(pe)], axis=-1)
    return dict(
        e_in=e_in,
        w_kpe2=jnp.concatenate([w_kpe, _rot_cols(w_kpe)], axis=-1),
        w_mg=e_in[:, :, o + QK_ROPE:],
        w_qb_ext=w_qb_ext.reshape(n_even, ql, heads * HEAD_QK),
        w_kvb=e_w_kvb.astype(BF16),
        e_out=e_w_out.astype(BF16),
        o_in=o_w_in.astype(BF16),
        w_s=o_w_s.astype(BF16),
        o_out=o_w_out.astype(BF16),
    )


def _trunk(x, *, batch, seq, cond_row, cs, cache, mod, norm_g, w, e_conv_w, e_q_norm_g, e_kv_norm_g,
           o_ln_g, o_ln_b, o_b_s, final_g, heads, emit):
    n_tok, d = x.shape
    depth = mod.shape[0]
    cw = e_conv_w.shape[2]
    ql = e_q_norm_g.shape[1]
    kvl = e_kv_norm_g.shape[1]
    assert seq & (seq - 1) == 0, "sequence length must be a power of two"
    assert (4 * cw) % ql == 0 and (4 * cw + ql) % kvl == 0 and cw == heads * V_HEAD
    tm = _tile(n_tok, 512)
    tml = _tile(n_tok, 1024)
    scale = float((QK_NOPE + QK_ROPE) ** -0.5 * math.log2(math.e))
    row_of = cond_row
    mods = [mod[l].reshape(mod.shape[1], 1, mod.shape[2]) for l in range(depth)]
    b_s = o_b_s[..., None]
    ckvs, kpes = [], []
    h = _first_h(x, mods[0], norm_g[0][None, :], tm=tm, row_of=row_of)
    for l in range(depth):
        i = l // 2
        if l % 2 == 0:
            y1 = _conv_branch(h, w["e_in"], i, e_conv_w[i], tm=tml, tn=_tile(cw, 256), seq=seq)
            q = _q_proj(h, w["e_in"], i, (4 * cw) // ql, e_q_norm_g[i][None, :], w["w_qb_ext"], cs, tm=tm,
                        seq=seq, heads=heads, scale=scale)
            kv = _kv_proj(h, w["e_in"], i, (4 * cw + ql) // kvl, e_kv_norm_g[i][None, :], w["w_kpe2"][i],
                          w["w_kvb"], cs, tm=tm, seq=seq, heads=heads, emit=emit)
            if emit:
                ckvs.append(kv[2])
                kpes.append(kv[3])
            gate = _gate(h, w["w_mg"][i], tm=tm, tn=_tile(w["w_mg"].shape[2], 512))
            if cache is None:
                y2 = _attention(q, kv[0], kv[1], gate, batch=batch, seq=seq, heads=heads)
            else:
                c_ckv, c_kpe = cache
                past = c_ckv.shape[2]
                kpe_pad = jnp.pad(c_kpe[:, i].reshape(batch * past, QK_ROPE),
                                  ((0, 0), (0, LANES - QK_ROPE))).astype(BF16)
                kc, vc = _cache_kv(c_ckv[:, i].reshape(batch * past, kvl), kpe_pad, w["w_kvb"], i,
                                   tm=_tile(batch * past, 512), heads=heads)
                y2 = _attention_pipelined(q, kc, vc, kv[0], kv[1], gate, batch=batch, seq=seq, past=past,
                                          heads=heads, tq=_tile(seq // 2, 512), nsub=2,
                                          kc_size=_tile(math.gcd(past, seq), 512))
            c2, w_out = 0, w["e_out"]
        else:
            vn = _v_ln(h, w["o_in"], i, o_ln_g[i][None, :], o_ln_b[i][None, :], tm=tm,
                       tnv=_tile(o_ln_g.shape[1], 512))
            y1 = y2 = _spatial_gate(h, w["o_in"], i, vn, w["w_s"], b_s, tm=tml)
            c2, w_out = 1, w["o_out"]
        if l + 1 < depth:
            x, h = _out_proj(y1, y2, 0, c2, w_out, i, x, mods[l], norm_g[l + 1][None, :], mods[l + 1],
                             tm=tm, tn=_tile(d, 512), row_of=row_of)
        else:
            x = _out_proj(y1, y2, 0, c2, w_out, i, x, mods[l], final_g[None, :], None,
                          tm=tm, tn=_tile(d, 512), row_of=row_of)
    return x, ckvs, kpes


def kernel(x_prompt, x_sample, cache_ckv, cache_kpe, c, c_ctx, norm_g, w_ada, b_ada, e_w_in, e_conv_w,
           e_q_norm_g, e_w_qb, e_kv_norm_g, e_w_kvb, e_w_out, o_w_in, o_ln_g, o_ln_b, o_w_s, o_b_s, o_w_out,
           final_g):
    batch, seq, d = x_prompt.shape
    dec_batch, dec_seq, _ = x_sample.shape
    cw = e_conv_w.shape[2]
    ql = e_q_norm_g.shape[1]
    kvl = e_kv_norm_g.shape[1]
    heads = e_w_qb.shape[2] // (QK_NOPE + QK_ROPE)

    rows = -(-(dec_batch + 1) // SUBLANES) * SUBLANES
    cond = jnp.concatenate([c, c_ctx[None, :], jnp.zeros((rows - dec_batch - 1, d), F32)], axis=0)
    mod = _ada(cond, w_ada, b_ada)

    w = _prep_weights(e_w_in, e_w_qb, e_w_kvb, e_w_out, o_w_in, o_w_s, o_w_out, cw, ql, kvl, heads)
    shared = dict(mod=mod, norm_g=norm_g, w=w, e_conv_w=e_conv_w, e_q_norm_g=e_q_norm_g,
                  e_kv_norm_g=e_kv_norm_g, o_ln_g=o_ln_g, o_ln_b=o_ln_b, o_b_s=o_b_s,
                  final_g=final_g, heads=heads)

    y_prompt, ckvs, kpes = _trunk(x_prompt.reshape(batch * seq, d), batch=batch, seq=seq,
                                  cond_row=lambda r: dec_batch, cs=None, cache=None, emit=True, **shared)
    y_sample, _, _ = _trunk(x_sample.reshape(dec_batch * dec_seq, d), batch=dec_batch, seq=dec_seq,
                            cond_row=lambda r: r // dec_seq, cs=_rope_table(dec_seq),
                            cache=(cache_ckv, cache_kpe), emit=False, **shared)
    new_ckv = jnp.stack([t.reshape(batch, seq, kvl) for t in ckvs], axis=1)
    new_kpe = jnp.stack([t.reshape(batch, seq, QK_ROPE) for t in kpes], axis=1)
    return (y_prompt.reshape(batch, seq, d), y_sample.reshape(dec_batch, dec_seq, d), new_ckv, new_kpe)
```

```python
import functools
import math

import jax
import jax.numpy as jnp
import numpy as np
from jax import lax
from jax.experimental import pallas as pl
from jax.experimental.pallas import tpu as pltpu

F32 = jnp.float32
BF16 = jnp.bfloat16

EPS = 1e-6
QK_NOPE = 128
QK_ROPE = 64
V_HEAD = 128
HEAD_QK = 256
GRID_W = 64
ROPE_BASE = 10000.0

V7X_VMEM_BYTES = 64 * 1024 * 1024
VMEM_LIMIT = V7X_VMEM_BYTES - 8 * 1024 * 1024
SUBLANES = 8
PACKED_ROWS = 16
LANES = 128


def _silu(x):
    return x * jax.nn.sigmoid(x)


def _dot(a, b):
    return jnp.dot(a, b, preferred_element_type=F32)


def _rms(x, g):
    return x * lax.rsqrt(jnp.mean(x * x, axis=-1, keepdims=True) + EPS) * g


def _params(*sem):
    return pltpu.CompilerParams(dimension_semantics=sem, vmem_limit_bytes=VMEM_LIMIT)


def _ada_kernel(c_ref, w_ref, b_ref, o_ref):
    a = _silu(c_ref[...]).astype(BF16)
    o_ref[0] = _dot(a, w_ref[0].astype(BF16)) + b_ref[0]


def _ada(cond, w_ada, b_ada):
    depth, d, n = w_ada.shape
    rows = cond.shape[0]
    tn = 768 if n % 768 == 0 else n
    return pl.pallas_call(
        _ada_kernel,
        out_shape=jax.ShapeDtypeStruct((depth, rows, n), F32),
        grid=(depth, n // tn),
        in_specs=[
            pl.BlockSpec((rows, d), lambda l, j: (0, 0)),
            pl.BlockSpec((1, d, tn), lambda l, j: (l, 0, j)),
            pl.BlockSpec((1, 1, tn), lambda l, j: (l, 0, j)),
        ],
        out_specs=pl.BlockSpec((1, rows, tn), lambda l, j: (l, 0, j)),
        compiler_params=_params("parallel", "parallel"),
        name="ada_mod",
    )(cond, w_ada, b_ada.reshape(depth, 1, n))


def _norm_mod(x, g, sc, sh):
    return _rms(x, g) * (1.0 + sc) + sh


def _norm_mod_kernel(x_ref, ng_ref, sh_ref, sc_ref, h_ref):
    h_ref[...] = _norm_mod(x_ref[...], ng_ref[...], sc_ref[0], sh_ref[0]).astype(BF16)


def _first_h(x, mod_l, norm_g, *, tm, row_of):
    n_tok, d = x.shape
    return pl.pallas_call(
        _norm_mod_kernel,
        out_shape=jax.ShapeDtypeStruct((n_tok, d), BF16),
        grid=(n_tok // tm,),
        in_specs=[
            pl.BlockSpec((tm, d), lambda i: (i, 0)),
            pl.BlockSpec((1, d), lambda i: (0, 0)),
            pl.BlockSpec((1, 1, d), lambda i: (row_of(i * tm), 0, 0)),
            pl.BlockSpec((1, 1, d), lambda i: (row_of(i * tm), 0, 1)),
        ],
        out_specs=pl.BlockSpec((tm, d), lambda i: (i, 0)),
        compiler_params=_params("parallel"),
        name="norm_mod",
    )(x, norm_g, mod_l, mod_l)


def _conv_kernel(h_ref, hp_ref, hn_ref, wb_ref, wc_ref, wx_ref, wg_ref, cw_ref, y_ref, hx_scr, *, tm, seq):
    i = pl.program_id(0)
    j = pl.program_id(1)

    @pl.when(j == 0)
    def _():
        hx_scr[0:tm] = h_ref[...]
        hx_scr[tm:tm + PACKED_ROWS] = hp_ref[...]
        hx_scr[tm + PACKED_ROWS:tm + 2 * PACKED_ROWS] = hn_ref[...]

    he = hx_scr[...]
    pe = _dot(he, wc_ref[...]) * _dot(he, wx_ref[...])
    p = pe[:tm]
    p_before = pe[tm + PACKED_ROWS - 1:tm + PACKED_ROWS]
    p_after = pe[tm + PACKED_ROWS:tm + PACKED_ROWS + 1]
    hm = h_ref[...]
    cb = _dot(hm, wb_ref[...])
    cg = _dot(hm, wg_ref[...])

    row = lax.broadcasted_iota(jnp.int32, p.shape, 0)
    pos = (i * tm + row) & (seq - 1)
    prev = jnp.where(row == 0, p_before, pltpu.roll(p, 1, 0))
    prev = jnp.where(pos == 0, 0.0, prev)
    nxt = jnp.where(row == tm - 1, p_after, pltpu.roll(p, tm - 1, 0))
    nxt = jnp.where(pos == seq - 1, 0.0, nxt)
    cw = cw_ref[...]
    conv = prev * cw[0:1] + p * cw[1:2] + nxt * cw[2:3]
    y_ref[...] = (cb * conv * _silu(cg)).astype(BF16)


def _conv_branch(h, w_in, li, conv_w, *, tm, tn, seq):
    n_tok, d = h.shape
    cw = conv_w.shape[1]
    nj = cw // tn
    n16 = n_tok // PACKED_ROWS
    t16 = tm // PACKED_ROWS
    wspec = lambda grp: pl.BlockSpec((None, d, tn), lambda i, j: (li, 0, grp * nj + j))
    return pl.pallas_call(
        functools.partial(_conv_kernel, tm=tm, seq=seq),
        out_shape=jax.ShapeDtypeStruct((n_tok, cw), BF16),
        grid=(n_tok // tm, nj),
        in_specs=[
            pl.BlockSpec((tm, d), lambda i, j: (i, 0)),
            pl.BlockSpec((PACKED_ROWS, d), lambda i, j: (jnp.maximum(i * t16 - 1, 0), 0)),
            pl.BlockSpec((PACKED_ROWS, d), lambda i, j: (jnp.minimum((i + 1) * t16, n16 - 1), 0)),
            wspec(0), wspec(1), wspec(2), wspec(3),
            pl.BlockSpec((3, tn), lambda i, j: (0, j)),
        ],
        out_specs=pl.BlockSpec((tm, tn), lambda i, j: (i, j)),
        scratch_shapes=[pltpu.VMEM((tm + 2 * PACKED_ROWS, d), BF16)],
        compiler_params=_params("parallel", "arbitrary"),
        name="even_conv",
    )(h, h, h, w_in, w_in, w_in, w_in, conv_w)


def _rope_pair(v, cs):
    t = v * cs
    return t + pltpu.roll(t, QK_ROPE, 1)


def _q_kernel(*refs, heads, rope, scale):
    if rope:
        h_ref, wqa_ref, qg_ref, wqbt_ref, cst_ref, qt_ref = refs
    else:
        h_ref, wqa_ref, qg_ref, wqbt_ref, qt_ref = refs
    qa = _dot(h_ref[...], wqa_ref[...])
    qnt = _rms(qa, qg_ref[...]).T.astype(BF16)
    zeros = jnp.zeros((HEAD_QK - QK_NOPE - QK_ROPE, qnt.shape[1]), BF16)
    for hd in range(heads):
        base = hd * HEAD_QK
        qt = _dot(wqbt_ref[base:base + HEAD_QK, :], qnt)
        pe = qt[QK_NOPE:]
        if rope:
            t = pe * cst_ref[...]
            pe = t[:QK_ROPE] + t[QK_ROPE:]
        else:
            pe = pe[:QK_ROPE]
        qt_ref[base:base + QK_NOPE, :] = (qt[:QK_NOPE] * scale).astype(BF16)
        qt_ref[base + QK_NOPE:base + QK_NOPE + QK_ROPE, :] = (pe * scale).astype(BF16)
        qt_ref[base + QK_NOPE + QK_ROPE:base + HEAD_QK, :] = zeros


def _q_proj(h, w_in, li, qa_block, q_norm_g, w_qbt, cst, *, tm, seq, heads, scale):
    n_tok, d = h.shape
    ql = q_norm_g.shape[1]
    rope = cst is not None
    in_specs = [
        pl.BlockSpec((tm, d), lambda i: (i, 0)),
        pl.BlockSpec((None, d, ql), lambda i: (li, 0, qa_block), pipeline_mode=pl.Buffered(1)),
        pl.BlockSpec((1, ql), lambda i: (0, 0)),
        pl.BlockSpec((None, heads * HEAD_QK, ql), lambda i: (li, 0, 0), pipeline_mode=pl.Buffered(1)),
    ]
    args = [h, w_in, q_norm_g, w_qbt]
    if rope:
        nb = seq // tm
        in_specs.append(pl.BlockSpec((LANES, tm), lambda i: (0, i % nb)))
        args.append(cst)
    return pl.pallas_call(
        functools.partial(_q_kernel, heads=heads, rope=rope, scale=scale),
        out_shape=jax.ShapeDtypeStruct((heads * HEAD_QK, n_tok), BF16),
        grid=(n_tok // tm,),
        in_specs=in_specs,
        out_specs=pl.BlockSpec((heads * HEAD_QK, tm), lambda i: (0, i)),
        compiler_params=_params("parallel"),
        name="mla_q_proj",
    )(*args)


HEAD_GROUP = 4


def _write_kv(ckv, kper, wk_ref, wvt_ref, k_ref, vt_ref, heads):
    cb = ckv.astype(BF16)
    cbt = ckv.T.astype(BF16)
    gw = HEAD_GROUP * V_HEAD
    for g in range(heads // HEAD_GROUP):
        kg = _dot(cb, wk_ref[:, g * gw:(g + 1) * gw])
        for j in range(HEAD_GROUP):
            base = (g * HEAD_GROUP + j) * HEAD_QK
            k_ref[:, base:base + QK_NOPE] = kg[:, j * QK_NOPE:(j + 1) * QK_NOPE].astype(BF16)
            k_ref[:, base + QK_NOPE:base + HEAD_QK] = kper
        vt_ref[g * gw:(g + 1) * gw, :] = _dot(wvt_ref[g * gw:(g + 1) * gw, :], cbt).astype(BF16)


def _kv_kernel(*refs, heads, rope, emit):
    refs = list(refs)
    h_ref, wckv_ref, kg_ref, wkpe_ref, wk_ref, wvt_ref = refs[:6]
    rest = refs[6:]
    cs_ref = rest.pop(0) if rope else None
    k_ref, vt_ref = rest[:2]
    h = h_ref[...]
    ckv = _rms(_dot(h, wckv_ref[...]), kg_ref[...])
    kp = _dot(h, wkpe_ref[...])
    if emit:
        ckv_ref, kpe_ref = rest[2:]
        ckv_ref[...] = ckv
        kpe_ref[...] = kp[:, :QK_ROPE]
    kk = _rope_pair(kp, cs_ref[...]) if rope else kp
    lane = lax.broadcasted_iota(jnp.int32, kk.shape, 1)
    kper = jnp.where(lane < QK_ROPE, kk, 0.0).astype(BF16)
    _write_kv(ckv, kper, wk_ref, wvt_ref, k_ref, vt_ref, heads)


def _kv_proj(h, w_in, li, ckv_block, kv_norm_g, w_kpe2, w_k, w_vt, cs, *, tm, seq, heads, emit):
    n_tok, d = h.shape
    kvl = kv_norm_g.shape[1]
    rope = cs is not None
    in_specs = [
        pl.BlockSpec((tm, d), lambda i: (i, 0)),
        pl.BlockSpec((None, d, kvl), lambda i: (li, 0, ckv_block), pipeline_mode=pl.Buffered(1)),
        pl.BlockSpec((1, kvl), lambda i: (0, 0)),
        pl.BlockSpec((d, LANES), lambda i: (0, 0)),
        pl.BlockSpec((None, kvl, heads * QK_NOPE), lambda i: (li, 0, 0), pipeline_mode=pl.Buffered(1)),
        pl.BlockSpec((None, heads * V_HEAD, kvl), lambda i: (li, 0, 0), pipeline_mode=pl.Buffered(1)),
    ]
    args = [h, w_in, kv_norm_g, w_kpe2, w_k, w_vt]
    if rope:
        nb = seq // tm
        in_specs.append(pl.BlockSpec((tm, LANES), lambda i: (i % nb, 0)))
        args.append(cs)
    out_shape = [jax.ShapeDtypeStruct((n_tok, heads * HEAD_QK), BF16),
                 jax.ShapeDtypeStruct((heads * V_HEAD, n_tok), BF16)]
    out_specs = [pl.BlockSpec((tm, heads * HEAD_QK), lambda i: (i, 0)),
                 pl.BlockSpec((heads * V_HEAD, tm), lambda i: (0, i))]
    if emit:
        out_shape += [jax.ShapeDtypeStruct((n_tok, kvl), F32), jax.ShapeDtypeStruct((n_tok, QK_ROPE), F32)]
        out_specs += [pl.BlockSpec((tm, kvl), lambda i: (i, 0)), pl.BlockSpec((tm, QK_ROPE), lambda i: (i, 0))]
    return pl.pallas_call(
        functools.partial(_kv_kernel, heads=heads, rope=rope, emit=emit),
        out_shape=tuple(out_shape),
        grid=(n_tok // tm,),
        in_specs=in_specs,
        out_specs=tuple(out_specs),
        compiler_params=_params("parallel"),
        name="mla_kv_proj",
    )(*args)


def _cache_kv_kernel(c_ref, kpe_ref, wk_ref, wvt_ref, k_ref, vt_ref, *, heads):
    _write_kv(c_ref[...], kpe_ref[...], wk_ref, wvt_ref, k_ref, vt_ref, heads)


def _cache_kv(ckv, kpe_pad, w_k, w_vt, li, *, tm, heads):
    n, kvl = ckv.shape
    return pl.pallas_call(
        functools.partial(_cache_kv_kernel, heads=heads),
        out_shape=(jax.ShapeDtypeStruct((n, heads * HEAD_QK), BF16),
                   jax.ShapeDtypeStruct((heads * V_HEAD, n), BF16)),
        grid=(n // tm,),
        in_specs=[
            pl.BlockSpec((tm, kvl), lambda i: (i, 0)),
            pl.BlockSpec((tm, LANES), lambda i: (i, 0)),
            pl.BlockSpec((None, kvl, heads * QK_NOPE), lambda i: (li, 0, 0), pipeline_mode=pl.Buffered(1)),
            pl.BlockSpec((None, heads * V_HEAD, kvl), lambda i: (li, 0, 0), pipeline_mode=pl.Buffered(1)),
        ],
        out_specs=(pl.BlockSpec((tm, heads * HEAD_QK), lambda i: (i, 0)),
                   pl.BlockSpec((heads * V_HEAD, tm), lambda i: (0, i))),
        compiler_params=_params("parallel"),
        name="mla_cache_kv",
    )(ckv, kpe_pad, w_k, w_vt)


def _gate_kernel(h_ref, w_ref, o_ref, *, tn):
    h = h_ref[...]
    for c in range(o_ref.shape[1] // tn):
        sl = slice(c * tn, (c + 1) * tn)
        o_ref[:, sl] = _silu(_dot(h, w_ref[:, sl])).astype(BF16)


def _gate(h, w, *, tm, tn):
    n_tok, d = h.shape
    n = w.shape[1]
    return pl.pallas_call(
        functools.partial(_gate_kernel, tn=tn),
        out_shape=jax.ShapeDtypeStruct((n_tok, n), BF16),
        grid=(n_tok // tm,),
        in_specs=[pl.BlockSpec((tm, d), lambda i: (i, 0)), pl.BlockSpec((d, n), lambda i: (0, 0))],
        out_specs=pl.BlockSpec((tm, n), lambda i: (i, 0)),
        compiler_params=_params("parallel"),
        name="mla_gate",
    )(h, w)


def _fold8(x, op):
    return op(x.reshape(x.shape[0] // SUBLANES, SUBLANES, x.shape[1]), axis=0)


def _attn_kernel(qt_ref, k_ref, vt_ref, gate_ref, o_ref, *, heads):
    for hd in range(heads):
        qk = slice(hd * HEAD_QK, (hd + 1) * HEAD_QK)
        cols = slice(hd * V_HEAD, (hd + 1) * V_HEAD)
        st = _dot(k_ref[:, qk], qt_ref[qk, :])
        pt = jnp.exp2(st - jnp.max(st, axis=0, keepdims=True))
        ot = _dot(vt_ref[cols, :], pt.astype(BF16)) / jnp.sum(pt, axis=0, keepdims=True)
        o_ref[:, cols] = (ot.T * gate_ref[:, cols].astype(F32)).astype(BF16)


def _attention(qt, k, vt, gate, *, batch, seq, heads):
    n_tok = k.shape[0]
    return pl.pallas_call(
        functools.partial(_attn_kernel, heads=heads),
        out_shape=jax.ShapeDtypeStruct((n_tok, heads * V_HEAD), BF16),
        grid=(batch,),
        in_specs=[
            pl.BlockSpec((heads * HEAD_QK, seq), lambda b: (0, b)),
            pl.BlockSpec((seq, heads * HEAD_QK), lambda b: (b, 0)),
            pl.BlockSpec((heads * V_HEAD, seq), lambda b: (0, b)),
            pl.BlockSpec((seq, heads * V_HEAD), lambda b: (b, 0)),
        ],
        out_specs=pl.BlockSpec((seq, heads * V_HEAD), lambda b: (b, 0)),
        compiler_params=_params("parallel"),
        name="mla_attention",
    )(qt, k, vt, gate)


def _attn_pipe_kernel(qt_ref, kc_ref, ko_ref, vtc_ref, vto_ref, gate_ref, o_ref, s_scr, m_scr, *, kc, nsub):
    @pl.when(jnp.logical_and(pl.program_id(0) == 0, pl.program_id(1) == 0))
    def _():
        s_scr[...] = jnp.zeros_like(s_scr)
        m_scr[...] = jnp.zeros_like(m_scr)

    tq = qt_ref.shape[1] // nsub
    half = tq // 2
    for u in range(nsub):
        qcols = slice(u * tq, (u + 1) * tq)
        qa = qt_ref[:, u * tq:u * tq + half]
        qb = qt_ref[:, u * tq + half:(u + 1) * tq]
        m_old = m_scr[u][0:1]
        acc = l_acc = m_acc = None
        off = 0
        for k_ref, vt_ref in ((kc_ref, vtc_ref), (ko_ref, vto_ref)):
            for c in range(k_ref.shape[0] // kc):
                rows = slice(c * kc, (c + 1) * kc)
                srows = slice(off, off + kc)
                pt = jnp.exp2(s_scr[u, srows, :] - m_old)
                lsum = _fold8(pt, jnp.sum)
                l_acc = lsum if l_acc is None else l_acc + lsum
                pv = _dot(vt_ref[:, rows], pt.astype(BF16))
                acc = pv if acc is None else acc + pv
                k_c = k_ref[rows, :]
                sa = _dot(k_c, qa)
                sb = _dot(k_c, qb)
                s_scr[u, srows, 0:half] = sa
                s_scr[u, srows, half:tq] = sb
                mx = jnp.concatenate([_fold8(sa, jnp.max), _fold8(sb, jnp.max)], axis=1)
                m_acc = mx if m_acc is None else jnp.maximum(m_acc, mx)
                off += kc
        m_scr[u] = jnp.broadcast_to(jnp.max(m_acc, axis=0, keepdims=True), (SUBLANES, tq))
        ot = acc / jnp.sum(l_acc, axis=0, keepdims=True)
        o_ref[qcols, :] = (ot.T * gate_ref[qcols, :].astype(F32)).astype(BF16)


def _attention_pipelined(qt, kc, vtc, ko, vto, gate, *, batch, seq, past, heads, tq, nsub, kc_size):
    n_tok = ko.shape[0]
    tsub, tq = tq, tq * nsub
    nq = seq // tq
    jobs = heads * nq
    jq = lambda g: jnp.minimum(g, jobs - 1)
    jp = lambda g: jnp.maximum(g - 1, 0)
    return pl.pallas_call(
        functools.partial(_attn_pipe_kernel, kc=kc_size, nsub=nsub),
        out_shape=jax.ShapeDtypeStruct((n_tok, heads * V_HEAD), BF16),
        grid=(batch, jobs + 1),
        in_specs=[
            pl.BlockSpec((HEAD_QK, tq), lambda b, g: (jq(g) // nq, b * nq + jq(g) % nq)),
            pl.BlockSpec((past, HEAD_QK), lambda b, g: (b, jq(g) // nq)),
            pl.BlockSpec((seq, HEAD_QK), lambda b, g: (b, jq(g) // nq)),
            pl.BlockSpec((V_HEAD, past), lambda b, g: (jp(g) // nq, b)),
            pl.BlockSpec((V_HEAD, seq), lambda b, g: (jp(g) // nq, b)),
            pl.BlockSpec((tq, V_HEAD), lambda b, g: (b * nq + jp(g) % nq, jp(g) // nq)),
        ],
        out_specs=pl.BlockSpec((tq, V_HEAD), lambda b, g: (b * nq + jp(g) % nq, jp(g) // nq)),
        scratch_shapes=[pltpu.VMEM((nsub, past + seq, tsub), F32), pltpu.VMEM((nsub, SUBLANES, tsub), F32)],
        compiler_params=_params("arbitrary", "arbitrary"),
        name="mla_attention_pipelined",
    )(qt, kc, ko, vtc, vto, gate)


def _out_kernel(*refs, tn, last):
    if last:
        y1_ref, y2_ref, w1_ref, w2_ref, x_ref, g_ref, ng_ref, o_ref = refs
    else:
        y1_ref, y2_ref, w1_ref, w2_ref, x_ref, g_ref, ng_ref, sh_ref, sc_ref, o_ref, h_ref = refs
    d = x_ref.shape[1]
    y1 = y1_ref[...]
    y2 = y2_ref[...]
    g = g_ref[0]
    ss = None
    for c in range(d // tn):
        sl = slice(c * tn, (c + 1) * tn)
        acc = _dot(y1, w1_ref[:, sl]) + _dot(y2, w2_ref[:, sl])
        xn = x_ref[:, sl] + g[:, sl] * acc
        o_ref[:, sl] = xn
        part = jnp.sum(xn * xn, axis=-1, keepdims=True)
        ss = part if ss is None else ss + part
    inv = lax.rsqrt(ss / d + EPS)
    for c in range(d // tn):
        sl = slice(c * tn, (c + 1) * tn)
        normed = o_ref[:, sl] * inv * ng_ref[:, sl]
        if last:
            o_ref[:, sl] = normed
        else:
            h_ref[:, sl] = (normed * (1.0 + sc_ref[0][:, sl]) + sh_ref[0][:, sl]).astype(BF16)


def _out_proj(y1, y2, c1, c2, w_out, li, x, mod_l, next_g, mod_next, *, tm, tn, row_of):
    n_tok, d = x.shape
    kh = w_out.shape[1] // 2
    last = mod_next is None
    in_specs = [
        pl.BlockSpec((tm, kh), lambda i: (i, c1)),
        pl.BlockSpec((tm, kh), lambda i: (i, c2)),
        pl.BlockSpec((None, kh, d), lambda i: (li, 0, 0), pipeline_mode=pl.Buffered(1)),
        pl.BlockSpec((None, kh, d), lambda i: (li, 1, 0), pipeline_mode=pl.Buffered(1)),
        pl.BlockSpec((tm, d), lambda i: (i, 0)),
        pl.BlockSpec((1, 1, d), lambda i: (row_of(i * tm), 0, 2)),
        pl.BlockSpec((1, d), lambda i: (0, 0)),
    ]
    args = [y1, y2, w_out, w_out, x, mod_l, next_g]
    out_shape = [jax.ShapeDtypeStruct((n_tok, d), F32)]
    out_specs = [pl.BlockSpec((tm, d), lambda i: (i, 0))]
    if not last:
        in_specs += [pl.BlockSpec((1, 1, d), lambda i: (row_of(i * tm), 0, 0)),
                     pl.BlockSpec((1, 1, d), lambda i: (row_of(i * tm), 0, 1))]
        args += [mod_next, mod_next]
        out_shape.append(jax.ShapeDtypeStruct((n_tok, d), BF16))
        out_specs.append(pl.BlockSpec((tm, d), lambda i: (i, 0)))
    res = pl.pallas_call(
        functools.partial(_out_kernel, tn=tn, last=last),
        out_shape=tuple(out_shape),
        grid=(n_tok // tm,),
        in_specs=in_specs,
        out_specs=tuple(out_specs),
        compiler_params=_params("parallel"),
        name="out_proj",
    )(*args)
    return res[0] if last else res


def _v_ln_kernel(h_ref, wv_ref, lg_ref, lb_ref, vn_ref, vacc, mu_scr, *, nj, tnv):
    @pl.when(pl.program_id(0) == 0)
    def _():
        vacc[...] = jnp.zeros_like(vacc)
        mu_scr[...] = jnp.zeros_like(mu_scr)

    width = nj * tnv
    mu = mu_scr[:, 0:1]
    sq = functools.reduce(
        jnp.add, [jnp.sum((vacc[t] - mu) * (vacc[t] - mu), axis=-1, keepdims=True) for t in range(nj)])
    inv = lax.rsqrt(sq / width + EPS)
    h = h_ref[...]
    tot = None
    for t in range(nj):
        sl = slice(t * tnv, (t + 1) * tnv)
        vn_ref[:, sl] = ((vacc[t] - mu) * inv * lg_ref[:, sl] + lb_ref[:, sl]).astype(BF16)
        v_new = _dot(h, wv_ref[:, sl])
        vacc[t] = v_new
        part = jnp.sum(v_new, axis=-1, keepdims=True)
        tot = part if tot is None else tot + part
    mu_scr[...] = jnp.broadcast_to(tot / width, mu_scr.shape)


def _v_ln(h, w_in, li, ln_g, ln_b, *, tm, tnv):
    n_tok, d = h.shape
    sgw = ln_g.shape[1]
    nj = sgw // tnv
    n = n_tok // tm
    return pl.pallas_call(
        functools.partial(_v_ln_kernel, nj=nj, tnv=tnv),
        out_shape=jax.ShapeDtypeStruct((n_tok, sgw), BF16),
        grid=(n + 1,),
        in_specs=[
            pl.BlockSpec((tm, d), lambda i: (jnp.minimum(i, n - 1), 0)),
            pl.BlockSpec((None, d, sgw), lambda i: (li, 0, 1), pipeline_mode=pl.Buffered(1)),
            pl.BlockSpec((1, sgw), lambda i: (0, 0)),
            pl.BlockSpec((1, sgw), lambda i: (0, 0)),
        ],
        out_specs=pl.BlockSpec((tm, sgw), lambda i: (jnp.maximum(i - 1, 0), 0)),
        scratch_shapes=[pltpu.VMEM((nj, tm, tnv), F32), pltpu.VMEM((tm, LANES), F32)],
        compiler_params=_params("arbitrary"),
        name="odd_v_ln",
    )(h, w_in, ln_g, ln_b)


def _sg_kernel(h_ref, wu_ref, wg_ref, vn_ref, ws_ref, bs_ref, y_ref, *, tm, chunk):
    h = h_ref[...]
    u = _dot(h, wu_ref[...])
    g = _dot(h, wg_ref[...])
    ws = ws_ref[...]
    bs = bs_ref[...]
    for c in range(tm // chunk):
        sl = slice(c * chunk, (c + 1) * chunk)
        vs = _dot(ws, vn_ref[sl, :]) + bs
        y_ref[sl, :] = (u[sl] * vs * _silu(g[sl])).astype(BF16)


def _spatial_gate(h, w_in, li, vn, w_s, b_s, *, tm):
    n_tok, d = h.shape
    _, groups, chunk, _ = w_s.shape
    sgw = vn.shape[1]
    tn = sgw // groups
    return pl.pallas_call(
        functools.partial(_sg_kernel, tm=tm, chunk=chunk),
        out_shape=jax.ShapeDtypeStruct((n_tok, sgw), BF16),
        grid=(n_tok // tm, groups),
        in_specs=[
            pl.BlockSpec((tm, d), lambda i, j: (i, 0)),
            pl.BlockSpec((None, d, tn), lambda i, j: (li, 0, j)),
            pl.BlockSpec((None, d, tn), lambda i, j: (li, 0, 2 * groups + j)),
            pl.BlockSpec((tm, tn), lambda i, j: (i, j)),
            pl.BlockSpec((None, None, chunk, chunk), lambda i, j: (li, j, 0, 0)),
            pl.BlockSpec((None, None, chunk, 1), lambda i, j: (li, j, 0, 0)),
        ],
        out_specs=pl.BlockSpec((tm, tn), lambda i, j: (i, j)),
        compiler_params=_params("parallel", "arbitrary"),
        name="odd_spatial_gate",
    )(h, w_in, w_in, vn, w_s, b_s)


def _rot_cols(w):
    shp = w.shape
    w4 = w.reshape(shp[:-1] + (2, 2, QK_ROPE // 4))
    return jnp.stack([-w4[..., 1, :], w4[..., 0, :]], axis=-2).reshape(shp)


def _rope_table(n):
    f32 = np.float32
    rows = n // GRID_W
    row = np.repeat(np.arange(rows, dtype=f32), GRID_W)
    col = np.tile(np.arange(GRID_W, dtype=f32), rows)
    n_freq = QK_ROPE // 4
    inv = np.power(f32(ROPE_BASE), -np.arange(n_freq, dtype=f32) / f32(n_freq)).astype(f32)
    ar = row[:, None] * inv
    ac = col[:, None] * inv
    ang = np.concatenate([ar, ar, ac, ac], axis=-1).astype(f32)
    cs = np.concatenate([np.cos(ang), np.sin(ang)], axis=-1).astype(f32)
    return jnp.asarray(cs), jnp.asarray(np.ascontiguousarray(cs.T))


def _tile(n, pref):
    return pref if n % pref == 0 else n


def _prep_weights(e_w_in, e_w_qb, e_w_kvb, e_w_out, o_w_in, o_w_s, o_w_out, cw, ql, kvl, heads):
    n_even = e_w_in.shape[0]
    e_in = e_w_in.astype(BF16)
    o = 4 * cw + ql + kvl
    w_kpe = e_in[:, :, o:o + QK_ROPE]
    wq = e_w_qb.astype(BF16).reshape(n_even, ql, heads, QK_NOPE + QK_ROPE)
    pe = wq[..., QK_NOPE:]
    w_qb_ext = jnp.concatenate([wq[..., :QK_NOPE], pe, _rot_cols(pe)], axis=-1)
    w_kvb = e_w_kvb.astype(BF16).reshape(n_even, kvl, heads, 2, V_HEAD)
    return dict(
        e_in=e_in,
        w_kpe2=jnp.concatenate([w_kpe, _rot_cols(w_kpe)], axis=-1),
        w_mg=e_in[:, :, o + QK_ROPE:],
        w_qbt=w_qb_ext.reshape(n_even, ql, heads * HEAD_QK).transpose(0, 2, 1),
        w_k=w_kvb[:, :, :, 0, :].reshape(n_even, kvl, heads * QK_NOPE),
        w_vt=w_kvb[:, :, :, 1, :].reshape(n_even, kvl, heads * V_HEAD).transpose(0, 2, 1),
        e_out=e_w_out.astype(BF16),
        o_in=o_w_in.astype(BF16),
        w_s=o_w_s.astype(BF16),
        o_out=o_w_out.astype(BF16),
    )


def _trunk(x, *, batch, seq, cond_row, cs, cache, mod, norm_g, w, e_conv_w, e_q_norm_g, e_kv_norm_g,
           o_ln_g, o_ln_b, o_b_s, final_g, heads, emit):
    n_tok, d = x.shape
    depth = mod.shape[0]
    cw = e_conv_w.shape[2]
    ql = e_q_norm_g.shape[1]
    kvl = e_kv_norm_g.shape[1]
    assert seq & (seq - 1) == 0, "sequence length must be a power of two"
    cs, cst = (None, None) if cs is None else cs
    assert heads % HEAD_GROUP == 0 and (4 * cw) % ql == 0 and (4 * cw + ql) % kvl == 0 and cw == heads * V_HEAD
    tm = _tile(n_tok, 512)
    tml = _tile(n_tok, 1024)
    scale = float((QK_NOPE + QK_ROPE) ** -0.5 * math.log2(math.e))
    row_of = cond_row
    mods = [mod[l].reshape(mod.shape[1], 1, mod.shape[2]) for l in range(depth)]
    b_s = o_b_s[..., None]
    ckvs, kpes = [], []
    h = _first_h(x, mods[0], norm_g[0][None, :], tm=tm, row_of=row_of)
    for l in range(depth):
        i = l // 2
        if l % 2 == 0:
            y1 = _conv_branch(h, w["e_in"], i, e_conv_w[i], tm=tml, tn=_tile(cw, 256), seq=seq)
            q = _q_proj(h, w["e_in"], i, (4 * cw) // ql, e_q_norm_g[i][None, :], w["w_qbt"],
                        cst, tm=tm, seq=seq, heads=heads, scale=scale)
            kv = _kv_proj(h, w["e_in"], i, (4 * cw + ql) // kvl, e_kv_norm_g[i][None, :], w["w_kpe2"][i],
                          w["w_k"], w["w_vt"], cs, tm=tm, seq=seq, heads=heads, emit=emit)
            if emit:
                ckvs.append(kv[2])
                kpes.append(kv[3])
            gate = _gate(h, w["w_mg"][i], tm=tm, tn=_tile(w["w_mg"].shape[2], 512))
            if cache is None:
                y2 = _attention(q, kv[0], kv[1], gate, batch=batch, seq=seq, heads=heads)
            else:
                c_ckv, c_kpe = cache
                past = c_ckv.shape[2]
                kpe_pad = jnp.pad(c_kpe[:, i].reshape(batch * past, QK_ROPE),
                                  ((0, 0), (0, LANES - QK_ROPE))).astype(BF16)
                kc, vc = _cache_kv(c_ckv[:, i].reshape(batch * past, kvl), kpe_pad, w["w_k"], w["w_vt"], i,
                                   tm=_tile(batch * past, 512), heads=heads)
                y2 = _attention_pipelined(q, kc, vc, kv[0], kv[1], gate, batch=batch, seq=seq, past=past,
                                          heads=heads, tq=_tile(seq // 2, 512), nsub=2,
                                          kc_size=_tile(math.gcd(past, seq), 512))
            c2, w_out = 0, w["e_out"]
        else:
            vn = _v_ln(h, w["o_in"], i, o_ln_g[i][None, :], o_ln_b[i][None, :], tm=tm,
                       tnv=_tile(o_ln_g.shape[1], 512))
            y1 = y2 = _spatial_gate(h, w["o_in"], i, vn, w["w_s"], b_s, tm=tml)
            c2, w_out = 1, w["o_out"]
        if l + 1 < depth:
            x, h = _out_proj(y1, y2, 0, c2, w_out, i, x, mods[l], norm_g[l + 1][None, :], mods[l + 1],
                             tm=tm, tn=_tile(d, 512), row_of=row_of)
        else:
            x = _out_proj(y1, y2, 0, c2, w_out, i, x, mods[l], final_g[None, :], None,
                          tm=tm, tn=_tile(d, 512), row_of=row_of)
    return x, ckvs, kpes


def kernel(x_prompt, x_sample, cache_ckv, cache_kpe, c, c_ctx, norm_g, w_ada, b_ada, e_w_in, e_conv_w,
           e_q_norm_g, e_w_qb, e_kv_norm_g, e_w_kvb, e_w_out, o_w_in, o_ln_g, o_ln_b, o_w_s, o_b_s, o_w_out,
           final_g):
    batch, seq, d = x_prompt.shape
    dec_batch, dec_seq, _ = x_sample.shape
    cw = e_conv_w.shape[2]
    ql = e_q_norm_g.shape[1]
    kvl = e_kv_norm_g.shape[1]
    heads = e_w_qb.shape[2] // (QK_NOPE + QK_ROPE)

    rows = -(-(dec_batch + 1) // SUBLANES) * SUBLANES
    cond = jnp.concatenate([c, c_ctx[None, :], jnp.zeros((rows - dec_batch - 1, d), F32)], axis=0)
    mod = _ada(cond, w_ada, b_ada)

    w = _prep_weights(e_w_in, e_w_qb, e_w_kvb, e_w_out, o_w_in, o_w_s, o_w_out, cw, ql, kvl, heads)
    shared = dict(mod=mod, norm_g=norm_g, w=w, e_conv_w=e_conv_w, e_q_norm_g=e_q_norm_g,
                  e_kv_norm_g=e_kv_norm_g, o_ln_g=o_ln_g, o_ln_b=o_ln_b, o_b_s=o_b_s,
                  final_g=final_g, heads=heads)

    y_prompt, ckvs, kpes = _trunk(x_prompt.reshape(batch * seq, d), batch=batch, seq=seq,
                                  cond_row=lambda r: dec_batch, cs=None, cache=None, emit=True, **shared)
    y_sample, _, _ = _trunk(x_sample.reshape(dec_batch * dec_seq, d), batch=dec_batch, seq=dec_seq,
                            cond_row=lambda r: r // dec_seq, cs=_rope_table(dec_seq),
                            cache=(cache_ckv, cache_kpe), emit=False, **shared)
    new_ckv = jnp.stack([t.reshape(batch, seq, kvl) for t in ckvs], axis=1)
    new_kpe = jnp.stack([t.reshape(batch, seq, QK_ROPE) for t in kpes], axis=1)
    return (y_prompt.reshape(batch, seq, d), y_sample.reshape(dec_batch, dec_seq, d), new_ckv, new_kpe)
```

```python
import functools
import math

import jax
import jax.numpy as jnp
import numpy as np
from jax import lax
from jax.experimental import pallas as pl
from jax.experimental.pallas import tpu as pltpu

F32 = jnp.float32
BF16 = jnp.bfloat16

EPS = 1e-6
QK_NOPE = 128
QK_ROPE = 64
V_HEAD = 128
HEAD_QK = 256
GRID_W = 64
ROPE_BASE = 10000.0

V7X_VMEM_BYTES = 64 * 1024 * 1024
VMEM_LIMIT = V7X_VMEM_BYTES - 8 * 1024 * 1024
SUBLANES = 8
PACKED_ROWS = 16
LANES = 128


def _silu(x):
    return x * jax.nn.sigmoid(x)


def _dot(a, b):
    return jnp.dot(a, b, preferred_element_type=F32)


def _rms(x, g):
    return x * lax.rsqrt(jnp.mean(x * x, axis=-1, keepdims=True) + EPS) * g


def _params(*sem):
    return pltpu.CompilerParams(dimension_semantics=sem, vmem_limit_bytes=VMEM_LIMIT)


def _ada_kernel(c_ref, w_ref, b_ref, o_ref):
    a = _silu(c_ref[...]).astype(BF16)
    o_ref[0] = _dot(a, w_ref[0].astype(BF16)) + b_ref[0]


def _ada(cond, w_ada, b_ada):
    depth, d, n = w_ada.shape
    rows = cond.shape[0]
    tn = 768 if n % 768 == 0 else n
    return pl.pallas_call(
        _ada_kernel,
        out_shape=jax.ShapeDtypeStruct((depth, rows, n), F32),
        grid=(depth, n // tn),
        in_specs=[
            pl.BlockSpec((rows, d), lambda l, j: (0, 0)),
            pl.BlockSpec((1, d, tn), lambda l, j: (l, 0, j)),
            pl.BlockSpec((1, 1, tn), lambda l, j: (l, 0, j)),
        ],
        out_specs=pl.BlockSpec((1, rows, tn), lambda l, j: (l, 0, j)),
        compiler_params=_params("parallel", "parallel"),
        name="ada_mod",
    )(cond, w_ada, b_ada.reshape(depth, 1, n))


def _norm_mod(x, g, sc, sh):
    return _rms(x, g) * (1.0 + sc) + sh


def _norm_mod_kernel(x_ref, ng_ref, sh_ref, sc_ref, h_ref):
    h_ref[...] = _norm_mod(x_ref[...], ng_ref[...], sc_ref[0], sh_ref[0]).astype(BF16)


def _first_h(x, mod_l, norm_g, *, tm, row_of):
    n_tok, d = x.shape
    return pl.pallas_call(
        _norm_mod_kernel,
        out_shape=jax.ShapeDtypeStruct((n_tok, d), BF16),
        grid=(n_tok // tm,),
        in_specs=[
            pl.BlockSpec((tm, d), lambda i: (i, 0)),
            pl.BlockSpec((1, d), lambda i: (0, 0)),
            pl.BlockSpec((1, 1, d), lambda i: (row_of(i * tm), 0, 0)),
            pl.BlockSpec((1, 1, d), lambda i: (row_of(i * tm), 0, 1)),
        ],
        out_specs=pl.BlockSpec((tm, d), lambda i: (i, 0)),
        compiler_params=_params("parallel"),
        name="norm_mod",
    )(x, norm_g, mod_l, mod_l)


def _conv_kernel(h_ref, hp_ref, hn_ref, wb_ref, wc_ref, wx_ref, wg_ref, cw_ref, y_ref, hx_scr, *, tm, seq, nsub):
    i = pl.program_id(0)
    j = pl.program_id(1)

    @pl.when(j == 0)
    def _():
        hx_scr[0:tm] = h_ref[...]
        hx_scr[tm:tm + PACKED_ROWS] = hp_ref[...]
        hx_scr[tm + PACKED_ROWS:tm + 2 * PACKED_ROWS] = hn_ref[...]

    he = hx_scr[...]
    hm = h_ref[...]
    ts = y_ref.shape[1] // nsub
    for s in range(nsub):
        cols = slice(s * ts, (s + 1) * ts)
        pe = _dot(he, wc_ref[:, cols]) * _dot(he, wx_ref[:, cols])
        p = pe[:tm]
        p_before = pe[tm + PACKED_ROWS - 1:tm + PACKED_ROWS]
        p_after = pe[tm + PACKED_ROWS:tm + PACKED_ROWS + 1]
        cb = _dot(hm, wb_ref[:, cols])
        cg = _dot(hm, wg_ref[:, cols])

        row = lax.broadcasted_iota(jnp.int32, p.shape, 0)
        pos = (i * tm + row) & (seq - 1)
        prev = jnp.where(row == 0, p_before, pltpu.roll(p, 1, 0))
        prev = jnp.where(pos == 0, 0.0, prev)
        nxt = jnp.where(row == tm - 1, p_after, pltpu.roll(p, tm - 1, 0))
        nxt = jnp.where(pos == seq - 1, 0.0, nxt)
        cw = cw_ref[:, cols]
        conv = prev * cw[0:1] + p * cw[1:2] + nxt * cw[2:3]
        y_ref[:, cols] = (cb * conv * _silu(cg)).astype(BF16)


def _conv_branch(h, w_in, li, conv_w, *, tm, tn, seq, nsub):
    n_tok, d = h.shape
    cw = conv_w.shape[1]
    nj = cw // tn
    n16 = n_tok // PACKED_ROWS
    t16 = tm // PACKED_ROWS
    wspec = lambda grp: pl.BlockSpec((None, d, tn), lambda i, j: (li, 0, grp * nj + j))
    return pl.pallas_call(
        functools.partial(_conv_kernel, tm=tm, seq=seq, nsub=nsub),
        out_shape=jax.ShapeDtypeStruct((n_tok, cw), BF16),
        grid=(n_tok // tm, nj),
        in_specs=[
            pl.BlockSpec((tm, d), lambda i, j: (i, 0)),
            pl.BlockSpec((PACKED_ROWS, d), lambda i, j: (jnp.maximum(i * t16 - 1, 0), 0)),
            pl.BlockSpec((PACKED_ROWS, d), lambda i, j: (jnp.minimum((i + 1) * t16, n16 - 1), 0)),
            wspec(0), wspec(1), wspec(2), wspec(3),
            pl.BlockSpec((3, tn), lambda i, j: (0, j)),
        ],
        out_specs=pl.BlockSpec((tm, tn), lambda i, j: (i, j)),
        scratch_shapes=[pltpu.VMEM((tm + 2 * PACKED_ROWS, d), BF16)],
        compiler_params=_params("parallel", "arbitrary"),
        name="even_conv",
    )(h, h, h, w_in, w_in, w_in, w_in, conv_w)


def _rope_pair(v, cs):
    t = v * cs
    return t + pltpu.roll(t, QK_ROPE, 1)


def _q_kernel(*refs, heads, rope, scale):
    if rope:
        h_ref, wqa_ref, qg_ref, wqbt_ref, cst_ref, qt_ref = refs
    else:
        h_ref, wqa_ref, qg_ref, wqbt_ref, qt_ref = refs
    qa = _dot(h_ref[...], wqa_ref[...])
    qnt = _rms(qa, qg_ref[...]).T.astype(BF16)
    zeros = jnp.zeros((HEAD_QK - QK_NOPE - QK_ROPE, qnt.shape[1]), BF16)
    for hd in range(heads):
        base = hd * HEAD_QK
        qt = _dot(wqbt_ref[base:base + HEAD_QK, :], qnt)
        pe = qt[QK_NOPE:]
        if rope:
            t = pe * cst_ref[...]
            pe = t[:QK_ROPE] + t[QK_ROPE:]
        else:
            pe = pe[:QK_ROPE]
        qt_ref[base:base + QK_NOPE, :] = (qt[:QK_NOPE] * scale).astype(BF16)
        qt_ref[base + QK_NOPE:base + QK_NOPE + QK_ROPE, :] = (pe * scale).astype(BF16)
        qt_ref[base + QK_NOPE + QK_ROPE:base + HEAD_QK, :] = zeros


def _q_proj(h, w_in, li, qa_block, q_norm_g, w_qbt, cst, *, tm, seq, heads, scale):
    n_tok, d = h.shape
    ql = q_norm_g.shape[1]
    rope = cst is not None
    in_specs = [
        pl.BlockSpec((tm, d), lambda i: (i, 0)),
        pl.BlockSpec((None, d, ql), lambda i: (li, 0, qa_block), pipeline_mode=pl.Buffered(1)),
        pl.BlockSpec((1, ql), lambda i: (0, 0)),
        pl.BlockSpec((None, heads * HEAD_QK, ql), lambda i: (li, 0, 0), pipeline_mode=pl.Buffered(1)),
    ]
    args = [h, w_in, q_norm_g, w_qbt]
    if rope:
        nb = seq // tm
        in_specs.append(pl.BlockSpec((LANES, tm), lambda i: (0, i % nb)))
        args.append(cst)
    return pl.pallas_call(
        functools.partial(_q_kernel, heads=heads, rope=rope, scale=scale),
        out_shape=jax.ShapeDtypeStruct((heads * HEAD_QK, n_tok), BF16),
        grid=(n_tok // tm,),
        in_specs=in_specs,
        out_specs=pl.BlockSpec((heads * HEAD_QK, tm), lambda i: (0, i)),
        compiler_params=_params("parallel"),
        name="mla_q_proj",
    )(*args)


HEAD_GROUP = 4


def _write_kv(ckv, kper, wk_ref, wvt_ref, k_ref, vt_ref, heads):
    cb = ckv.astype(BF16)
    cbt = ckv.T.astype(BF16)
    gw = HEAD_GROUP * V_HEAD
    for g in range(heads // HEAD_GROUP):
        kg = _dot(cb, wk_ref[:, g * gw:(g + 1) * gw])
        for j in range(HEAD_GROUP):
            base = (g * HEAD_GROUP + j) * HEAD_QK
            k_ref[:, base:base + QK_NOPE] = kg[:, j * QK_NOPE:(j + 1) * QK_NOPE].astype(BF16)
            k_ref[:, base + QK_NOPE:base + HEAD_QK] = kper
        vt_ref[g * gw:(g + 1) * gw, :] = _dot(wvt_ref[g * gw:(g + 1) * gw, :], cbt).astype(BF16)


def _kv_kernel(*refs, heads, rope, emit):
    refs = list(refs)
    h_ref, wckv_ref, kg_ref, wkpe_ref, wk_ref, wvt_ref = refs[:6]
    rest = refs[6:]
    cs_ref = rest.pop(0) if rope else None
    k_ref, vt_ref = rest[:2]
    h = h_ref[...]
    ckv = _rms(_dot(h, wckv_ref[...]), kg_ref[...])
    kp = _dot(h, wkpe_ref[...])
    if emit:
        ckv_ref, kpe_ref = rest[2:]
        ckv_ref[...] = ckv
        kpe_ref[...] = kp[:, :QK_ROPE]
    kk = _rope_pair(kp, cs_ref[...]) if rope else kp
    lane = lax.broadcasted_iota(jnp.int32, kk.shape, 1)
    kper = jnp.where(lane < QK_ROPE, kk, 0.0).astype(BF16)
    _write_kv(ckv, kper, wk_ref, wvt_ref, k_ref, vt_ref, heads)


def _kv_proj(h, w_in, li, ckv_block, kv_norm_g, w_kpe2, w_k, w_vt, cs, *, tm, seq, heads, emit):
    n_tok, d = h.shape
    kvl = kv_norm_g.shape[1]
    rope = cs is not None
    in_specs = [
        pl.BlockSpec((tm, d), lambda i: (i, 0)),
        pl.BlockSpec((None, d, kvl), lambda i: (li, 0, ckv_block), pipeline_mode=pl.Buffered(1)),
        pl.BlockSpec((1, kvl), lambda i: (0, 0)),
        pl.BlockSpec((d, LANES), lambda i: (0, 0)),
        pl.BlockSpec((None, kvl, heads * QK_NOPE), lambda i: (li, 0, 0), pipeline_mode=pl.Buffered(1)),
        pl.BlockSpec((None, heads * V_HEAD, kvl), lambda i: (li, 0, 0), pipeline_mode=pl.Buffered(1)),
    ]
    args = [h, w_in, kv_norm_g, w_kpe2, w_k, w_vt]
    if rope:
        nb = seq // tm
        in_specs.append(pl.BlockSpec((tm, LANES), lambda i: (i % nb, 0)))
        args.append(cs)
    out_shape = [jax.ShapeDtypeStruct((n_tok, heads * HEAD_QK), BF16),
                 jax.ShapeDtypeStruct((heads * V_HEAD, n_tok), BF16)]
    out_specs = [pl.BlockSpec((tm, heads * HEAD_QK), lambda i: (i, 0)),
                 pl.BlockSpec((heads * V_HEAD, tm), lambda i: (0, i))]
    if emit:
        out_shape += [jax.ShapeDtypeStruct((n_tok, kvl), F32), jax.ShapeDtypeStruct((n_tok, QK_ROPE), F32)]
        out_specs += [pl.BlockSpec((tm, kvl), lambda i: (i, 0)), pl.BlockSpec((tm, QK_ROPE), lambda i: (i, 0))]
    return pl.pallas_call(
        functools.partial(_kv_kernel, heads=heads, rope=rope, emit=emit),
        out_shape=tuple(out_shape),
        grid=(n_tok // tm,),
        in_specs=in_specs,
        out_specs=tuple(out_specs),
        compiler_params=_params("parallel"),
        name="mla_kv_proj",
    )(*args)


def _cache_kv_kernel(c_ref, kpe_ref, wk_ref, wvt_ref, k_ref, vt_ref, *, heads):
    _write_kv(c_ref[...], kpe_ref[...], wk_ref, wvt_ref, k_ref, vt_ref, heads)


def _cache_kv(ckv, kpe_pad, w_k, w_vt, li, *, tm, heads):
    n, kvl = ckv.shape
    return pl.pallas_call(
        functools.partial(_cache_kv_kernel, heads=heads),
        out_shape=(jax.ShapeDtypeStruct((n, heads * HEAD_QK), BF16),
                   jax.ShapeDtypeStruct((heads * V_HEAD, n), BF16)),
        grid=(n // tm,),
        in_specs=[
            pl.BlockSpec((tm, kvl), lambda i: (i, 0)),
            pl.BlockSpec((tm, LANES), lambda i: (i, 0)),
            pl.BlockSpec((None, kvl, heads * QK_NOPE), lambda i: (li, 0, 0), pipeline_mode=pl.Buffered(1)),
            pl.BlockSpec((None, heads * V_HEAD, kvl), lambda i: (li, 0, 0), pipeline_mode=pl.Buffered(1)),
        ],
        out_specs=(pl.BlockSpec((tm, heads * HEAD_QK), lambda i: (i, 0)),
                   pl.BlockSpec((heads * V_HEAD, tm), lambda i: (0, i))),
        compiler_params=_params("parallel"),
        name="mla_cache_kv",
    )(ckv, kpe_pad, w_k, w_vt)


def _gate_kernel(h_ref, w_ref, o_ref, *, tn):
    h = h_ref[...]
    for c in range(o_ref.shape[1] // tn):
        sl = slice(c * tn, (c + 1) * tn)
        o_ref[:, sl] = _silu(_dot(h, w_ref[:, sl])).astype(BF16)


def _gate(h, w, *, tm, tn):
    n_tok, d = h.shape
    n = w.shape[1]
    return pl.pallas_call(
        functools.partial(_gate_kernel, tn=tn),
        out_shape=jax.ShapeDtypeStruct((n_tok, n), BF16),
        grid=(n_tok // tm,),
        in_specs=[pl.BlockSpec((tm, d), lambda i: (i, 0)), pl.BlockSpec((d, n), lambda i: (0, 0))],
        out_specs=pl.BlockSpec((tm, n), lambda i: (i, 0)),
        compiler_params=_params("parallel"),
        name="mla_gate",
    )(h, w)


def _fold8(x, op):
    return op(x.reshape(x.shape[0] // SUBLANES, SUBLANES, x.shape[1]), axis=0)


def _attn_kernel(qt_ref, k_ref, vt_ref, gate_ref, o_ref, *, heads):
    for hd in range(heads):
        qk = slice(hd * HEAD_QK, (hd + 1) * HEAD_QK)
        cols = slice(hd * V_HEAD, (hd + 1) * V_HEAD)
        st = _dot(k_ref[:, qk], qt_ref[qk, :])
        pt = jnp.exp2(st - jnp.max(st, axis=0, keepdims=True))
        ot = _dot(vt_ref[cols, :], pt.astype(BF16)) / jnp.sum(pt, axis=0, keepdims=True)
        o_ref[:, cols] = (ot.T * gate_ref[:, cols].astype(F32)).astype(BF16)


def _attention(qt, k, vt, gate, *, batch, seq, heads):
    n_tok = k.shape[0]
    return pl.pallas_call(
        functools.partial(_attn_kernel, heads=heads),
        out_shape=jax.ShapeDtypeStruct((n_tok, heads * V_HEAD), BF16),
        grid=(batch,),
        in_specs=[
            pl.BlockSpec((heads * HEAD_QK, seq), lambda b: (0, b)),
            pl.BlockSpec((seq, heads * HEAD_QK), lambda b: (b, 0)),
            pl.BlockSpec((heads * V_HEAD, seq), lambda b: (0, b)),
            pl.BlockSpec((seq, heads * V_HEAD), lambda b: (b, 0)),
        ],
        out_specs=pl.BlockSpec((seq, heads * V_HEAD), lambda b: (b, 0)),
        compiler_params=_params("parallel"),
        name="mla_attention",
    )(qt, k, vt, gate)


def _attn_pipe_kernel(qt_ref, kc_ref, ko_ref, vtc_ref, vto_ref, gate_ref, o_ref, s_scr, m_scr, *, kc, nsub):
    @pl.when(jnp.logical_and(pl.program_id(0) == 0, pl.program_id(1) == 0))
    def _():
        s_scr[...] = jnp.zeros_like(s_scr)
        m_scr[...] = jnp.zeros_like(m_scr)

    tq = qt_ref.shape[1] // nsub
    half = tq // 2
    for u in range(nsub):
        qcols = slice(u * tq, (u + 1) * tq)
        qa = qt_ref[:, u * tq:u * tq + half]
        qb = qt_ref[:, u * tq + half:(u + 1) * tq]
        m_old = m_scr[u][0:1]
        acc = l_acc = m_acc = None
        off = 0
        for k_ref, vt_ref in ((kc_ref, vtc_ref), (ko_ref, vto_ref)):
            for c in range(k_ref.shape[0] // kc):
                rows = slice(c * kc, (c + 1) * kc)
                srows = slice(off, off + kc)
                pt = jnp.exp2(s_scr[u, srows, :] - m_old)
                lsum = _fold8(pt, jnp.sum)
                l_acc = lsum if l_acc is None else l_acc + lsum
                pv = _dot(vt_ref[:, rows], pt.astype(BF16))
                acc = pv if acc is None else acc + pv
                k_c = k_ref[rows, :]
                sa = _dot(k_c, qa)
                sb = _dot(k_c, qb)
                s_scr[u, srows, 0:half] = sa
                s_scr[u, srows, half:tq] = sb
                mx = jnp.concatenate([_fold8(sa, jnp.max), _fold8(sb, jnp.max)], axis=1)
                m_acc = mx if m_acc is None else jnp.maximum(m_acc, mx)
                off += kc
        m_scr[u] = jnp.broadcast_to(jnp.max(m_acc, axis=0, keepdims=True), (SUBLANES, tq))
        ot = acc / jnp.sum(l_acc, axis=0, keepdims=True)
        o_ref[qcols, :] = (ot.T * gate_ref[qcols, :].astype(F32)).astype(BF16)


def _attention_pipelined(qt, kc, vtc, ko, vto, gate, *, batch, seq, past, heads, tq, nsub, kc_size):
    n_tok = ko.shape[0]
    tsub, tq = tq, tq * nsub
    nq = seq // tq
    jobs = heads * nq
    jq = lambda g: jnp.minimum(g, jobs - 1)
    jp = lambda g: jnp.maximum(g - 1, 0)
    return pl.pallas_call(
        functools.partial(_attn_pipe_kernel, kc=kc_size, nsub=nsub),
        out_shape=jax.ShapeDtypeStruct((n_tok, heads * V_HEAD), BF16),
        grid=(batch, jobs + 1),
        in_specs=[
            pl.BlockSpec((HEAD_QK, tq), lambda b, g: (jq(g) // nq, b * nq + jq(g) % nq)),
            pl.BlockSpec((past, HEAD_QK), lambda b, g: (b, jq(g) // nq)),
            pl.BlockSpec((seq, HEAD_QK), lambda b, g: (b, jq(g) // nq)),
            pl.BlockSpec((V_HEAD, past), lambda b, g: (jp(g) // nq, b)),
            pl.BlockSpec((V_HEAD, seq), lambda b, g: (jp(g) // nq, b)),
            pl.BlockSpec((tq, V_HEAD), lambda b, g: (b * nq + jp(g) % nq, jp(g) // nq)),
        ],
        out_specs=pl.BlockSpec((tq, V_HEAD), lambda b, g: (b * nq + jp(g) % nq, jp(g) // nq)),
        scratch_shapes=[pltpu.VMEM((nsub, past + seq, tsub), F32), pltpu.VMEM((nsub, SUBLANES, tsub), F32)],
        compiler_params=_params("arbitrary", "arbitrary"),
        name="mla_attention_pipelined",
    )(qt, kc, ko, vtc, vto, gate)


def _out_kernel(*refs, tn, last, n):
    if last:
        y1_ref, y2_ref, w1_ref, w2_ref, x_ref, g_ref, ng_ref, o_ref, xs_scr, ss_scr = refs
    else:
        (y1_ref, y2_ref, w1_ref, w2_ref, x_ref, g_ref, ng_ref, sh_ref, sc_ref,
         xo_ref, h_ref, xs_scr, ss_scr) = refs
    i = pl.program_id(0)
    d = x_ref.shape[1]

    @pl.when(i == 0)
    def _():
        xs_scr[...] = jnp.zeros_like(xs_scr)
        ss_scr[...] = jnp.zeros_like(ss_scr)

    def norm_prev(sl, inv):
        normed = xs_scr[:, sl] * inv * ng_ref[:, sl]
        if last:
            o_ref[:, sl] = normed
        else:
            h_ref[:, sl] = (normed * (1.0 + sc_ref[0][:, sl]) + sh_ref[0][:, sl]).astype(BF16)

    @pl.when(i < n)
    def _():
        inv = lax.rsqrt(ss_scr[:, 0:1] / d + EPS)
        y1 = y1_ref[...]
        y2 = y2_ref[...]
        g = g_ref[0]
        ss = None
        for c in range(d // tn):
            sl = slice(c * tn, (c + 1) * tn)
            norm_prev(sl, inv)
            acc = _dot(y1, w1_ref[:, sl]) + _dot(y2, w2_ref[:, sl])
            xn = x_ref[:, sl] + g[:, sl] * acc
            if not last:
                xo_ref[:, sl] = xn
            xs_scr[:, sl] = xn
            part = jnp.sum(xn * xn, axis=-1, keepdims=True)
            ss = part if ss is None else ss + part
        ss_scr[...] = jnp.broadcast_to(ss, ss_scr.shape)

    @pl.when(i == n)
    def _():
        inv = lax.rsqrt(ss_scr[:, 0:1] / d + EPS)
        for c in range(d // tn):
            norm_prev(slice(c * tn, (c + 1) * tn), inv)


def _out_proj(y1, y2, c1, c2, w_out, li, x, mod_l, next_g, mod_next, *, tm, tn, row_of):
    n_tok, d = x.shape
    kh = w_out.shape[1] // 2
    last = mod_next is None
    n = n_tok // tm
    cur = lambda i: jnp.minimum(i, n - 1)
    prv = lambda i: jnp.maximum(i - 1, 0)
    in_specs = [
        pl.BlockSpec((tm, kh), lambda i: (cur(i), c1)),
        pl.BlockSpec((tm, kh), lambda i: (cur(i), c2)),
        pl.BlockSpec((None, kh, d), lambda i: (li, 0, 0), pipeline_mode=pl.Buffered(1)),
        pl.BlockSpec((None, kh, d), lambda i: (li, 1, 0), pipeline_mode=pl.Buffered(1)),
        pl.BlockSpec((tm, d), lambda i: (cur(i), 0)),
        pl.BlockSpec((1, 1, d), lambda i: (row_of(cur(i) * tm), 0, 2)),
        pl.BlockSpec((1, d), lambda i: (0, 0)),
    ]
    args = [y1, y2, w_out, w_out, x, mod_l, next_g]
    if last:
        out_shape = [jax.ShapeDtypeStruct((n_tok, d), F32)]
        out_specs = [pl.BlockSpec((tm, d), lambda i: (prv(i), 0))]
    else:
        in_specs += [pl.BlockSpec((1, 1, d), lambda i: (row_of(prv(i) * tm), 0, 0)),
                     pl.BlockSpec((1, 1, d), lambda i: (row_of(prv(i) * tm), 0, 1))]
        args += [mod_next, mod_next]
        out_shape = [jax.ShapeDtypeStruct((n_tok, d), F32), jax.ShapeDtypeStruct((n_tok, d), BF16)]
        out_specs = [pl.BlockSpec((tm, d), lambda i: (cur(i), 0)), pl.BlockSpec((tm, d), lambda i: (prv(i), 0))]
    res = pl.pallas_call(
        functools.partial(_out_kernel, tn=tn, last=last, n=n),
        out_shape=tuple(out_shape),
        grid=(n + 1,),
        in_specs=in_specs,
        out_specs=tuple(out_specs),
        scratch_shapes=[pltpu.VMEM((tm, d), F32), pltpu.VMEM((tm, LANES), F32)],
        compiler_params=_params("arbitrary"),
        name="out_proj",
    )(*args)
    return res[0] if last else res


def _v_ln_kernel(h_ref, wv_ref, lg_ref, lb_ref, vn_ref, vacc, mu_scr, *, nj, tnv, n):
    @pl.when(pl.program_id(0) == 0)
    def _():
        vacc[...] = jnp.zeros_like(vacc)
        mu_scr[...] = jnp.zeros_like(mu_scr)

    width = nj * tnv

    def prev_stats():
        mu = mu_scr[:, 0:1]
        sq = functools.reduce(
            jnp.add, [jnp.sum((vacc[t] - mu) * (vacc[t] - mu), axis=-1, keepdims=True) for t in range(nj)])
        return mu, lax.rsqrt(sq / width + EPS)

    def norm_prev(t, mu, inv):
        sl = slice(t * tnv, (t + 1) * tnv)
        vn_ref[:, sl] = ((vacc[t] - mu) * inv * lg_ref[:, sl] + lb_ref[:, sl]).astype(BF16)

    @pl.when(pl.program_id(0) < n)
    def _():
        mu, inv = prev_stats()
        h = h_ref[...]
        tot = None
        for t in range(nj):
            norm_prev(t, mu, inv)
            v_new = _dot(h, wv_ref[:, t * tnv:(t + 1) * tnv])
            vacc[t] = v_new
            part = jnp.sum(v_new, axis=-1, keepdims=True)
            tot = part if tot is None else tot + part
        mu_scr[...] = jnp.broadcast_to(tot / width, mu_scr.shape)

    @pl.when(pl.program_id(0) == n)
    def _():
        mu, inv = prev_stats()
        for t in range(nj):
            norm_prev(t, mu, inv)


def _v_ln(h, w_in, li, ln_g, ln_b, *, tm, tnv):
    n_tok, d = h.shape
    sgw = ln_g.shape[1]
    nj = sgw // tnv
    n = n_tok // tm
    return pl.pallas_call(
        functools.partial(_v_ln_kernel, nj=nj, tnv=tnv, n=n),
        out_shape=jax.ShapeDtypeStruct((n_tok, sgw), BF16),
        grid=(n + 1,),
        in_specs=[
            pl.BlockSpec((tm, d), lambda i: (jnp.minimum(i, n - 1), 0)),
            pl.BlockSpec((None, d, sgw), lambda i: (li, 0, 1), pipeline_mode=pl.Buffered(1)),
            pl.BlockSpec((1, sgw), lambda i: (0, 0)),
            pl.BlockSpec((1, sgw), lambda i: (0, 0)),
        ],
        out_specs=pl.BlockSpec((tm, sgw), lambda i: (jnp.maximum(i - 1, 0), 0)),
        scratch_shapes=[pltpu.VMEM((nj, tm, tnv), F32), pltpu.VMEM((tm, LANES), F32)],
        compiler_params=_params("arbitrary"),
        name="odd_v_ln",
    )(h, w_in, ln_g, ln_b)


def _sg_kernel(h_ref, wu_ref, wg_ref, vn_ref, ws_ref, bs_ref, y_ref, *, tm, chunk, gps):
    h = h_ref[...]
    gw = y_ref.shape[1] // gps
    for s in range(gps):
        cols = slice(s * gw, (s + 1) * gw)
        u = _dot(h, wu_ref[:, cols])
        g = _dot(h, wg_ref[:, cols])
        ws = ws_ref[s]
        bs = bs_ref[s]
        for c in range(tm // chunk):
            sl = slice(c * chunk, (c + 1) * chunk)
            vs = _dot(ws, vn_ref[sl, cols]) + bs
            y_ref[sl, cols] = (u[sl] * vs * _silu(g[sl])).astype(BF16)


def _spatial_gate(h, w_in, li, vn, w_s, b_s, *, tm, gps):
    n_tok, d = h.shape
    _, groups, chunk, _ = w_s.shape
    sgw = vn.shape[1]
    tn = gps * sgw // groups
    nj = groups // gps
    return pl.pallas_call(
        functools.partial(_sg_kernel, tm=tm, chunk=chunk, gps=gps),
        out_shape=jax.ShapeDtypeStruct((n_tok, sgw), BF16),
        grid=(n_tok // tm, nj),
        in_specs=[
            pl.BlockSpec((tm, d), lambda i, j: (i, 0)),
            pl.BlockSpec((None, d, tn), lambda i, j: (li, 0, j)),
            pl.BlockSpec((None, d, tn), lambda i, j: (li, 0, 2 * nj + j)),
            pl.BlockSpec((tm, tn), lambda i, j: (i, j)),
            pl.BlockSpec((None, gps, chunk, chunk), lambda i, j: (li, j, 0, 0)),
            pl.BlockSpec((None, gps, chunk, 1), lambda i, j: (li, j, 0, 0)),
        ],
        out_specs=pl.BlockSpec((tm, tn), lambda i, j: (i, j)),
        compiler_params=_params("parallel", "arbitrary"),
        name="odd_spatial_gate",
    )(h, w_in, w_in, vn, w_s, b_s)


def _rot_cols(w):
    shp = w.shape
    w4 = w.reshape(shp[:-1] + (2, 2, QK_ROPE // 4))
    return jnp.stack([-w4[..., 1, :], w4[..., 0, :]], axis=-2).reshape(shp)


def _rope_table(n):
    f32 = np.float32
    rows = n // GRID_W
    row = np.repeat(np.arange(rows, dtype=f32), GRID_W)
    col = np.tile(np.arange(GRID_W, dtype=f32), rows)
    n_freq = QK_ROPE // 4
    inv = np.power(f32(ROPE_BASE), -np.arange(n_freq, dtype=f32) / f32(n_freq)).astype(f32)
    ar = row[:, None] * inv
    ac = col[:, None] * inv
    ang = np.concatenate([ar, ar, ac, ac], axis=-1).astype(f32)
    cs = np.concatenate([np.cos(ang), np.sin(ang)], axis=-1).astype(f32)
    return jnp.asarray(cs), jnp.asarray(np.ascontiguousarray(cs.T))


def _tile(n, pref):
    return pref if n % pref == 0 else n


def _prep_weights(e_w_in, e_w_qb, e_w_kvb, e_w_out, o_w_in, o_w_s, o_w_out, cw, ql, kvl, heads):
    n_even = e_w_in.shape[0]
    e_in = e_w_in.astype(BF16)
    o = 4 * cw + ql + kvl
    w_kpe = e_in[:, :, o:o + QK_ROPE]
    n = np.arange(heads * HEAD_QK)
    hd, r = n // HEAD_QK, n % HEAD_QK
    j = r - QK_NOPE - QK_ROPE
    first = (j % (QK_ROPE // 2)) < QK_ROPE // 4
    src = np.where(j < 0, r, QK_NOPE + np.where(first, j + QK_ROPE // 4, j - QK_ROPE // 4))
    src = hd * (QK_NOPE + QK_ROPE) + src
    sign = np.where((j >= 0) & first, -1.0, 1.0).astype(np.float32)
    cols = lax.broadcasted_iota(jnp.int32, (heads * HEAD_QK, heads * (QK_NOPE + QK_ROPE)), 1)
    sel = jnp.where(cols == jnp.asarray(src, jnp.int32)[:, None], jnp.asarray(sign)[:, None], 0.0).astype(BF16)
    w_qbt = jnp.einsum("nc,lkc->lnk", sel, e_w_qb.astype(BF16), preferred_element_type=F32).astype(BF16)
    w_kvb = e_w_kvb.astype(BF16).reshape(n_even, kvl, heads, 2, V_HEAD)
    return dict(
        e_in=e_in,
        w_kpe2=jnp.concatenate([w_kpe, _rot_cols(w_kpe)], axis=-1),
        w_mg=e_in[:, :, o + QK_ROPE:],
        w_qbt=w_qbt,
        w_k=w_kvb[:, :, :, 0, :].reshape(n_even, kvl, heads * QK_NOPE),
        w_vt=w_kvb[:, :, :, 1, :].reshape(n_even, kvl, heads * V_HEAD).transpose(0, 2, 1),
        e_out=e_w_out.astype(BF16),
        o_in=o_w_in.astype(BF16),
        w_s=o_w_s.astype(BF16),
        o_out=o_w_out.astype(BF16),
    )


def _trunk(x, *, batch, seq, cond_row, cs, cache, mod, norm_g, w, e_conv_w, e_q_norm_g, e_kv_norm_g,
           o_ln_g, o_ln_b, o_b_s, final_g, heads, emit):
    n_tok, d = x.shape
    depth = mod.shape[0]
    cw = e_conv_w.shape[2]
    ql = e_q_norm_g.shape[1]
    kvl = e_kv_norm_g.shape[1]
    assert seq & (seq - 1) == 0, "sequence length must be a power of two"
    cs, cst = (None, None) if cs is None else cs
    assert heads % HEAD_GROUP == 0 and (4 * cw) % ql == 0 and (4 * cw + ql) % kvl == 0 and cw == heads * V_HEAD
    tm = _tile(n_tok, 512)
    tml = _tile(n_tok, 1024)
    scale = float((QK_NOPE + QK_ROPE) ** -0.5 * math.log2(math.e))
    row_of = cond_row
    mods = [mod[l].reshape(mod.shape[1], 1, mod.shape[2]) for l in range(depth)]
    b_s = o_b_s[..., None]
    ckvs, kpes = [], []
    h = _first_h(x, mods[0], norm_g[0][None, :], tm=tm, row_of=row_of)
    for l in range(depth):
        i = l // 2
        if l % 2 == 0:
            y1 = _conv_branch(h, w["e_in"], i, e_conv_w[i], tm=tml, tn=_tile(cw, 512), seq=seq,
                              nsub=2 if _tile(cw, 512) % 512 == 0 else 1)
            q = _q_proj(h, w["e_in"], i, (4 * cw) // ql, e_q_norm_g[i][None, :], w["w_qbt"],
                        cst, tm=tm, seq=seq, heads=heads, scale=scale)
            kv = _kv_proj(h, w["e_in"], i, (4 * cw + ql) // kvl, e_kv_norm_g[i][None, :], w["w_kpe2"][i],
                          w["w_k"], w["w_vt"], cs, tm=tm, seq=seq, heads=heads, emit=emit)
            if emit:
                ckvs.append(kv[2])
                kpes.append(kv[3])
            gate = _gate(h, w["w_mg"][i], tm=tm, tn=_tile(w["w_mg"].shape[2], 512))
            if cache is None:
                y2 = _attention(q, kv[0], kv[1], gate, batch=batch, seq=seq, heads=heads)
            else:
                c_ckv, c_kpe = cache
                past = c_ckv.shape[2]
                kpe_pad = jnp.pad(c_kpe[:, i].reshape(batch * past, QK_ROPE),
                                  ((0, 0), (0, LANES - QK_ROPE))).astype(BF16)
                kc, vc = _cache_kv(c_ckv[:, i].reshape(batch * past, kvl), kpe_pad, w["w_k"], w["w_vt"], i,
                                   tm=_tile(batch * past, 512), heads=heads)
                y2 = _attention_pipelined(q, kc, vc, kv[0], kv[1], gate, batch=batch, seq=seq, past=past,
                                          heads=heads, tq=_tile(seq // 2, 512), nsub=2,
                                          kc_size=_tile(math.gcd(past, seq), 512))
            c2, w_out = 0, w["e_out"]
        else:
            vn = _v_ln(h, w["o_in"], i, o_ln_g[i][None, :], o_ln_b[i][None, :], tm=tm,
                       tnv=_tile(o_ln_g.shape[1], 512))
            groups = w["w_s"].shape[1]
            y1 = y2 = _spatial_gate(h, w["o_in"], i, vn, w["w_s"], b_s, tm=tml, gps=4 if groups % 4 == 0 else 1)
            c2, w_out = 1, w["o_out"]
        if l + 1 < depth:
            x, h = _out_proj(y1, y2, 0, c2, w_out, i, x, mods[l], norm_g[l + 1][None, :], mods[l + 1],
                             tm=tm, tn=_tile(d, 512), row_of=row_of)
        else:
            x = _out_proj(y1, y2, 0, c2, w_out, i, x, mods[l], final_g[None, :], None,
                          tm=tm, tn=_tile(d, 512), row_of=row_of)
    return x, ckvs, kpes


def kernel(x_prompt, x_sample, cache_ckv, cache_kpe, c, c_ctx, norm_g, w_ada, b_ada, e_w_in, e_conv_w,
           e_q_norm_g, e_w_qb, e_kv_norm_g, e_w_kvb, e_w_out, o_w_in, o_ln_g, o_ln_b, o_w_s, o_b_s, o_w_out,
           final_g):
    batch, seq, d = x_prompt.shape
    dec_batch, dec_seq, _ = x_sample.shape
    cw = e_conv_w.shape[2]
    ql = e_q_norm_g.shape[1]
    kvl = e_kv_norm_g.shape[1]
    heads = e_w_qb.shape[2] // (QK_NOPE + QK_ROPE)

    rows = -(-(dec_batch + 1) // SUBLANES) * SUBLANES
    cond = jnp.concatenate([c, c_ctx[None, :], jnp.zeros((rows - dec_batch - 1, d), F32)], axis=0)
    mod = _ada(cond, w_ada, b_ada)

    w = _prep_weights(e_w_in, e_w_qb, e_w_kvb, e_w_out, o_w_in, o_w_s, o_w_out, cw, ql, kvl, heads)
    shared = dict(mod=mod, norm_g=norm_g, w=w, e_conv_w=e_conv_w, e_q_norm_g=e_q_norm_g,
                  e_kv_norm_g=e_kv_norm_g, o_ln_g=o_ln_g, o_ln_b=o_ln_b, o_b_s=o_b_s,
                  final_g=final_g, heads=heads)

    y_prompt, ckvs, kpes = _trunk(x_prompt.reshape(batch * seq, d), batch=batch, seq=seq,
                                  cond_row=lambda r: dec_batch, cs=None, cache=None, emit=True, **shared)
    y_sample, _, _ = _trunk(x_sample.reshape(dec_batch * dec_seq, d), batch=dec_batch, seq=dec_seq,
                            cond_row=lambda r: r // dec_seq, cs=_rope_table(dec_seq),
                            cache=(cache_ckv, cache_kpe), emit=False, **shared)
    new_ckv = jnp.stack([t.reshape(batch, seq, kvl) for t in ckvs], axis=1)
    new_kpe = jnp.stack([t.reshape(batch, seq, QK_ROPE) for t in kpes], axis=1)
    return (y_prompt.reshape(batch, seq, d), y_sample.reshape(dec_batch, dec_seq, d), new_ckv, new_kpe)
```

```python
import functools
import math

import jax
import jax.numpy as jnp
import numpy as np
from jax import lax
from jax.experimental import pallas as pl
from jax.experimental.pallas import tpu as pltpu

F32 = jnp.float32
BF16 = jnp.bfloat16

EPS = 1e-6
QK_NOPE = 128
QK_ROPE = 64
V_HEAD = 128
HEAD_QK = 256
GRID_W = 64
ROPE_BASE = 10000.0

V7X_VMEM_BYTES = 64 * 1024 * 1024
VMEM_LIMIT = V7X_VMEM_BYTES - 8 * 1024 * 1024
SUBLANES = 8
PACKED_ROWS = 16
LANES = 128


def _silu(x):
    return x * jax.nn.sigmoid(x)


def _dot(a, b):
    return jnp.dot(a, b, preferred_element_type=F32)


def _rms(x, g):
    return x * lax.rsqrt(jnp.mean(x * x, axis=-1, keepdims=True) + EPS) * g


def _params(*sem):
    return pltpu.CompilerParams(dimension_semantics=sem, vmem_limit_bytes=VMEM_LIMIT)


def _ada_kernel(c_ref, w_ref, b_ref, o_ref):
    a = _silu(c_ref[...]).astype(BF16)
    o_ref[0] = _dot(a, w_ref[0].astype(BF16)) + b_ref[0]


def _ada(cond, w_ada, b_ada):
    depth, d, n = w_ada.shape
    rows = cond.shape[0]
    tn = 768 if n % 768 == 0 else n
    return pl.pallas_call(
        _ada_kernel,
        out_shape=jax.ShapeDtypeStruct((depth, rows, n), F32),
        grid=(depth, n // tn),
        in_specs=[
            pl.BlockSpec((rows, d), lambda l, j: (0, 0)),
            pl.BlockSpec((1, d, tn), lambda l, j: (l, 0, j)),
            pl.BlockSpec((1, 1, tn), lambda l, j: (l, 0, j)),
        ],
        out_specs=pl.BlockSpec((1, rows, tn), lambda l, j: (l, 0, j)),
        compiler_params=_params("parallel", "parallel"),
        name="ada_mod",
    )(cond, w_ada, b_ada.reshape(depth, 1, n))


def _norm_mod(x, g, sc, sh):
    return _rms(x, g) * (1.0 + sc) + sh


def _norm_mod_kernel(x_ref, ng_ref, sh_ref, sc_ref, h_ref):
    h_ref[...] = _norm_mod(x_ref[...], ng_ref[...], sc_ref[0], sh_ref[0]).astype(BF16)


def _first_h(x, mod_l, norm_g, *, tm, row_of):
    n_tok, d = x.shape
    return pl.pallas_call(
        _norm_mod_kernel,
        out_shape=jax.ShapeDtypeStruct((n_tok, d), BF16),
        grid=(n_tok // tm,),
        in_specs=[
            pl.BlockSpec((tm, d), lambda i: (i, 0)),
            pl.BlockSpec((1, d), lambda i: (0, 0)),
            pl.BlockSpec((1, 1, d), lambda i: (row_of(i * tm), 0, 0)),
            pl.BlockSpec((1, 1, d), lambda i: (row_of(i * tm), 0, 1)),
        ],
        out_specs=pl.BlockSpec((tm, d), lambda i: (i, 0)),
        compiler_params=_params("parallel"),
        name="norm_mod",
    )(x, norm_g, mod_l, mod_l)


def _conv_kernel(h_ref, hp_ref, hn_ref, wb_ref, wc_ref, wx_ref, wg_ref, cw_ref, y_ref, hx_scr, *, tm, seq, nsub):
    i = pl.program_id(0)
    j = pl.program_id(1)

    @pl.when(j == 0)
    def _():
        hx_scr[0:tm] = h_ref[...]
        hx_scr[tm:tm + PACKED_ROWS] = hp_ref[...]
        hx_scr[tm + PACKED_ROWS:tm + 2 * PACKED_ROWS] = hn_ref[...]

    he = hx_scr[...]
    hm = h_ref[...]
    ts = y_ref.shape[1] // nsub
    for s in range(nsub):
        cols = slice(s * ts, (s + 1) * ts)
        pe = _dot(he, wc_ref[:, cols]) * _dot(he, wx_ref[:, cols])
        p = pe[:tm]
        p_before = pe[tm + PACKED_ROWS - 1:tm + PACKED_ROWS]
        p_after = pe[tm + PACKED_ROWS:tm + PACKED_ROWS + 1]
        cb = _dot(hm, wb_ref[:, cols])
        cg = _dot(hm, wg_ref[:, cols])

        row = lax.broadcasted_iota(jnp.int32, p.shape, 0)
        pos = (i * tm + row) & (seq - 1)
        prev = jnp.where(row == 0, p_before, pltpu.roll(p, 1, 0))
        prev = jnp.where(pos == 0, 0.0, prev)
        nxt = jnp.where(row == tm - 1, p_after, pltpu.roll(p, tm - 1, 0))
        nxt = jnp.where(pos == seq - 1, 0.0, nxt)
        cw = cw_ref[:, cols]
        conv = prev * cw[0:1] + p * cw[1:2] + nxt * cw[2:3]
        y_ref[:, cols] = (cb * conv * _silu(cg)).astype(BF16)


def _conv_branch(h, w_in, li, conv_w, *, tm, tn, seq, nsub):
    n_tok, d = h.shape
    cw = conv_w.shape[1]
    nj = cw // tn
    n16 = n_tok // PACKED_ROWS
    t16 = tm // PACKED_ROWS
    wspec = lambda grp: pl.BlockSpec((None, d, tn), lambda i, j: (li, 0, grp * nj + j))
    return pl.pallas_call(
        functools.partial(_conv_kernel, tm=tm, seq=seq, nsub=nsub),
        out_shape=jax.ShapeDtypeStruct((n_tok, cw), BF16),
        grid=(n_tok // tm, nj),
        in_specs=[
            pl.BlockSpec((tm, d), lambda i, j: (i, 0)),
            pl.BlockSpec((PACKED_ROWS, d), lambda i, j: (jnp.maximum(i * t16 - 1, 0), 0)),
            pl.BlockSpec((PACKED_ROWS, d), lambda i, j: (jnp.minimum((i + 1) * t16, n16 - 1), 0)),
            wspec(0), wspec(1), wspec(2), wspec(3),
            pl.BlockSpec((3, tn), lambda i, j: (0, j)),
        ],
        out_specs=pl.BlockSpec((tm, tn), lambda i, j: (i, j)),
        scratch_shapes=[pltpu.VMEM((tm + 2 * PACKED_ROWS, d), BF16)],
        compiler_params=_params("parallel", "arbitrary"),
        name="even_conv",
    )(h, h, h, w_in, w_in, w_in, w_in, conv_w)


def _rope_pair(v, cs):
    t = v * cs
    return t + pltpu.roll(t, QK_ROPE, 1)


def _q_kernel(*refs, heads, rope, scale):
    if rope:
        h_ref, wqa_ref, qg_ref, wqbt_ref, cst_ref, qt_ref = refs
    else:
        h_ref, wqa_ref, qg_ref, wqbt_ref, qt_ref = refs
    qa = _dot(h_ref[...], wqa_ref[...])
    qnt = _rms(qa, qg_ref[...]).T.astype(BF16)
    zeros = jnp.zeros((HEAD_QK - QK_NOPE - QK_ROPE, qnt.shape[1]), BF16)
    for hd in range(heads):
        base = hd * HEAD_QK
        qt = _dot(wqbt_ref[base:base + HEAD_QK, :], qnt)
        pe = qt[QK_NOPE:]
        if rope:
            t = pe * cst_ref[...]
            pe = t[:QK_ROPE] + t[QK_ROPE:]
        else:
            pe = pe[:QK_ROPE]
        qt_ref[base:base + QK_NOPE, :] = (qt[:QK_NOPE] * scale).astype(BF16)
        qt_ref[base + QK_NOPE:base + QK_NOPE + QK_ROPE, :] = (pe * scale).astype(BF16)
        qt_ref[base + QK_NOPE + QK_ROPE:base + HEAD_QK, :] = zeros


def _q_proj(h, w_in, li, qa_block, q_norm_g, w_qbt, cst, *, tm, seq, heads, scale):
    n_tok, d = h.shape
    ql = q_norm_g.shape[1]
    rope = cst is not None
    in_specs = [
        pl.BlockSpec((tm, d), lambda i: (i, 0)),
        pl.BlockSpec((None, d, ql), lambda i: (li, 0, qa_block), pipeline_mode=pl.Buffered(1)),
        pl.BlockSpec((1, ql), lambda i: (0, 0)),
        pl.BlockSpec((None, heads * HEAD_QK, ql), lambda i: (li, 0, 0), pipeline_mode=pl.Buffered(1)),
    ]
    args = [h, w_in, q_norm_g, w_qbt]
    if rope:
        nb = seq // tm
        in_specs.append(pl.BlockSpec((LANES, tm), lambda i: (0, i % nb)))
        args.append(cst)
    return pl.pallas_call(
        functools.partial(_q_kernel, heads=heads, rope=rope, scale=scale),
        out_shape=jax.ShapeDtypeStruct((heads * HEAD_QK, n_tok), BF16),
        grid=(n_tok // tm,),
        in_specs=in_specs,
        out_specs=pl.BlockSpec((heads * HEAD_QK, tm), lambda i: (0, i)),
        compiler_params=_params("parallel"),
        name="mla_q_proj",
    )(*args)


HEAD_GROUP = 4


def _write_kv(ckv, kper, wk_ref, wvt_ref, k_ref, vt_ref, heads):
    cb = ckv.astype(BF16)
    cbt = ckv.T.astype(BF16)
    gw = HEAD_GROUP * V_HEAD
    for g in range(heads // HEAD_GROUP):
        kg = _dot(cb, wk_ref[:, g * gw:(g + 1) * gw])
        for j in range(HEAD_GROUP):
            base = (g * HEAD_GROUP + j) * HEAD_QK
            k_ref[:, base:base + QK_NOPE] = kg[:, j * QK_NOPE:(j + 1) * QK_NOPE].astype(BF16)
            k_ref[:, base + QK_NOPE:base + HEAD_QK] = kper
        vt_ref[g * gw:(g + 1) * gw, :] = _dot(wvt_ref[g * gw:(g + 1) * gw, :], cbt).astype(BF16)


def _kv_kernel(*refs, heads, rope, emit):
    refs = list(refs)
    h_ref, wckv_ref, kg_ref, wkpe_ref, wk_ref, wvt_ref = refs[:6]
    rest = refs[6:]
    cs_ref = rest.pop(0) if rope else None
    k_ref, vt_ref = rest[:2]
    h = h_ref[...]
    ckv = _rms(_dot(h, wckv_ref[...]), kg_ref[...])
    kp = _dot(h, wkpe_ref[...])
    if emit:
        ckv_ref, kpe_ref = rest[2:]
        ckv_ref[...] = ckv
        kpe_ref[...] = kp[:, :QK_ROPE]
    kk = _rope_pair(kp, cs_ref[...]) if rope else kp
    lane = lax.broadcasted_iota(jnp.int32, kk.shape, 1)
    kper = jnp.where(lane < QK_ROPE, kk, 0.0).astype(BF16)
    _write_kv(ckv, kper, wk_ref, wvt_ref, k_ref, vt_ref, heads)


def _kv_proj(h, w_in, li, ckv_block, kv_norm_g, w_kpe2, w_k, w_vt, cs, *, tm, seq, heads, emit):
    n_tok, d = h.shape
    kvl = kv_norm_g.shape[1]
    rope = cs is not None
    in_specs = [
        pl.BlockSpec((tm, d), lambda i: (i, 0)),
        pl.BlockSpec((None, d, kvl), lambda i: (li, 0, ckv_block), pipeline_mode=pl.Buffered(1)),
        pl.BlockSpec((1, kvl), lambda i: (0, 0)),
        pl.BlockSpec((d, LANES), lambda i: (0, 0)),
        pl.BlockSpec((None, kvl, heads * QK_NOPE), lambda i: (li, 0, 0), pipeline_mode=pl.Buffered(1)),
        pl.BlockSpec((None, heads * V_HEAD, kvl), lambda i: (li, 0, 0), pipeline_mode=pl.Buffered(1)),
    ]
    args = [h, w_in, kv_norm_g, w_kpe2, w_k, w_vt]
    if rope:
        nb = seq // tm
        in_specs.append(pl.BlockSpec((tm, LANES), lambda i: (i % nb, 0)))
        args.append(cs)
    out_shape = [jax.ShapeDtypeStruct((n_tok, heads * HEAD_QK), BF16),
                 jax.ShapeDtypeStruct((heads * V_HEAD, n_tok), BF16)]
    out_specs = [pl.BlockSpec((tm, heads * HEAD_QK), lambda i: (i, 0)),
                 pl.BlockSpec((heads * V_HEAD, tm), lambda i: (0, i))]
    if emit:
        out_shape += [jax.ShapeDtypeStruct((n_tok, kvl), F32), jax.ShapeDtypeStruct((n_tok, QK_ROPE), F32)]
        out_specs += [pl.BlockSpec((tm, kvl), lambda i: (i, 0)), pl.BlockSpec((tm, QK_ROPE), lambda i: (i, 0))]
    return pl.pallas_call(
        functools.partial(_kv_kernel, heads=heads, rope=rope, emit=emit),
        out_shape=tuple(out_shape),
        grid=(n_tok // tm,),
        in_specs=in_specs,
        out_specs=tuple(out_specs),
        compiler_params=_params("parallel"),
        name="mla_kv_proj",
    )(*args)


def _cache_kv_kernel(c_ref, kpe_ref, wk_ref, wvt_ref, k_ref, vt_ref, *, heads):
    _write_kv(c_ref[...], kpe_ref[...], wk_ref, wvt_ref, k_ref, vt_ref, heads)


def _cache_kv(ckv, kpe_pad, w_k, w_vt, li, *, tm, heads):
    n, kvl = ckv.shape
    return pl.pallas_call(
        functools.partial(_cache_kv_kernel, heads=heads),
        out_shape=(jax.ShapeDtypeStruct((n, heads * HEAD_QK), BF16),
                   jax.ShapeDtypeStruct((heads * V_HEAD, n), BF16)),
        grid=(n // tm,),
        in_specs=[
            pl.BlockSpec((tm, kvl), lambda i: (i, 0)),
            pl.BlockSpec((tm, LANES), lambda i: (i, 0)),
            pl.BlockSpec((None, kvl, heads * QK_NOPE), lambda i: (li, 0, 0), pipeline_mode=pl.Buffered(1)),
            pl.BlockSpec((None, heads * V_HEAD, kvl), lambda i: (li, 0, 0), pipeline_mode=pl.Buffered(1)),
        ],
        out_specs=(pl.BlockSpec((tm, heads * HEAD_QK), lambda i: (i, 0)),
                   pl.BlockSpec((heads * V_HEAD, tm), lambda i: (0, i))),
        compiler_params=_params("parallel"),
        name="mla_cache_kv",
    )(ckv, kpe_pad, w_k, w_vt)


def _gate_kernel(h_ref, w_ref, o_ref, *, tn):
    h = h_ref[...]
    for c in range(o_ref.shape[1] // tn):
        sl = slice(c * tn, (c + 1) * tn)
        o_ref[:, sl] = _silu(_dot(h, w_ref[:, sl])).astype(BF16)


def _gate(h, w, *, tm, tn):
    n_tok, d = h.shape
    n = w.shape[1]
    return pl.pallas_call(
        functools.partial(_gate_kernel, tn=tn),
        out_shape=jax.ShapeDtypeStruct((n_tok, n), BF16),
        grid=(n_tok // tm,),
        in_specs=[pl.BlockSpec((tm, d), lambda i: (i, 0)), pl.BlockSpec((d, n), lambda i: (0, 0))],
        out_specs=pl.BlockSpec((tm, n), lambda i: (i, 0)),
        compiler_params=_params("parallel"),
        name="mla_gate",
    )(h, w)


def _fold8(x, op):
    return op(x.reshape(x.shape[0] // SUBLANES, SUBLANES, x.shape[1]), axis=0)


def _attn_kernel(qt_ref, k_ref, vt_ref, gate_ref, o_ref, *, heads):
    for hd in range(heads):
        qk = slice(hd * HEAD_QK, (hd + 1) * HEAD_QK)
        cols = slice(hd * V_HEAD, (hd + 1) * V_HEAD)
        st = _dot(k_ref[:, qk], qt_ref[qk, :])
        pt = jnp.exp2(st - jnp.max(st, axis=0, keepdims=True))
        ot = _dot(vt_ref[cols, :], pt.astype(BF16)) / jnp.sum(pt, axis=0, keepdims=True)
        o_ref[:, cols] = (ot.T * gate_ref[:, cols].astype(F32)).astype(BF16)


def _attention(qt, k, vt, gate, *, batch, seq, heads):
    n_tok = k.shape[0]
    return pl.pallas_call(
        functools.partial(_attn_kernel, heads=heads),
        out_shape=jax.ShapeDtypeStruct((n_tok, heads * V_HEAD), BF16),
        grid=(batch,),
        in_specs=[
            pl.BlockSpec((heads * HEAD_QK, seq), lambda b: (0, b)),
            pl.BlockSpec((seq, heads * HEAD_QK), lambda b: (b, 0)),
            pl.BlockSpec((heads * V_HEAD, seq), lambda b: (0, b)),
            pl.BlockSpec((seq, heads * V_HEAD), lambda b: (b, 0)),
        ],
        out_specs=pl.BlockSpec((seq, heads * V_HEAD), lambda b: (b, 0)),
        compiler_params=_params("parallel"),
        name="mla_attention",
    )(qt, k, vt, gate)


def _attn_pipe_kernel(qt_ref, kc_ref, ko_ref, vtc_ref, vto_ref, gate_ref, o_ref, s_scr, m_scr, *, kc, nsub):
    @pl.when(jnp.logical_and(pl.program_id(0) == 0, pl.program_id(1) == 0))
    def _():
        s_scr[...] = jnp.zeros_like(s_scr)
        m_scr[...] = jnp.zeros_like(m_scr)

    tq = qt_ref.shape[1] // nsub
    half = tq // 2
    for u in range(nsub):
        qcols = slice(u * tq, (u + 1) * tq)
        qa = qt_ref[:, u * tq:u * tq + half]
        qb = qt_ref[:, u * tq + half:(u + 1) * tq]
        m_old = m_scr[u][0:1]
        acc = l_acc = m_acc = None
        off = 0
        for k_ref, vt_ref in ((kc_ref, vtc_ref), (ko_ref, vto_ref)):
            for c in range(k_ref.shape[0] // kc):
                rows = slice(c * kc, (c + 1) * kc)
                srows = slice(off, off + kc)
                pt = jnp.exp2(s_scr[u, srows, :] - m_old)
                lsum = _fold8(pt, jnp.sum)
                l_acc = lsum if l_acc is None else l_acc + lsum
                pv = _dot(vt_ref[:, rows], pt.astype(BF16))
                acc = pv if acc is None else acc + pv
                k_c = k_ref[rows, :]
                sa = _dot(k_c, qa)
                sb = _dot(k_c, qb)
                s_scr[u, srows, 0:half] = sa
                s_scr[u, srows, half:tq] = sb
                mx = jnp.concatenate([_fold8(sa, jnp.max), _fold8(sb, jnp.max)], axis=1)
                m_acc = mx if m_acc is None else jnp.maximum(m_acc, mx)
                off += kc
        m_scr[u] = jnp.broadcast_to(jnp.max(m_acc, axis=0, keepdims=True), (SUBLANES, tq))
        ot = acc / jnp.sum(l_acc, axis=0, keepdims=True)
        o_ref[qcols, :] = (ot.T * gate_ref[qcols, :].astype(F32)).astype(BF16)


def _attention_pipelined(qt, kc, vtc, ko, vto, gate, *, batch, seq, past, heads, tq, nsub, kc_size):
    n_tok = ko.shape[0]
    tsub, tq = tq, tq * nsub
    nq = seq // tq
    jobs = heads * nq
    jq = lambda g: jnp.minimum(g, jobs - 1)
    jp = lambda g: jnp.maximum(g - 1, 0)
    return pl.pallas_call(
        functools.partial(_attn_pipe_kernel, kc=kc_size, nsub=nsub),
        out_shape=jax.ShapeDtypeStruct((n_tok, heads * V_HEAD), BF16),
        grid=(batch, jobs + 1),
        in_specs=[
            pl.BlockSpec((HEAD_QK, tq), lambda b, g: (jq(g) // nq, b * nq + jq(g) % nq)),
            pl.BlockSpec((past, HEAD_QK), lambda b, g: (b, jq(g) // nq)),
            pl.BlockSpec((seq, HEAD_QK), lambda b, g: (b, jq(g) // nq)),
            pl.BlockSpec((V_HEAD, past), lambda b, g: (jp(g) // nq, b)),
            pl.BlockSpec((V_HEAD, seq), lambda b, g: (jp(g) // nq, b)),
            pl.BlockSpec((tq, V_HEAD), lambda b, g: (b * nq + jp(g) % nq, jp(g) // nq)),
        ],
        out_specs=pl.BlockSpec((tq, V_HEAD), lambda b, g: (b * nq + jp(g) % nq, jp(g) // nq)),
        scratch_shapes=[pltpu.VMEM((nsub, past + seq, tsub), F32), pltpu.VMEM((nsub, SUBLANES, tsub), F32)],
        compiler_params=_params("arbitrary", "arbitrary"),
        name="mla_attention_pipelined",
    )(qt, kc, ko, vtc, vto, gate)


def _out_kernel(*refs, tn, last):
    if last:
        y1_ref, y2_ref, w1_ref, w2_ref, x_ref, g_ref, ng_ref, o_ref = refs
    else:
        y1_ref, y2_ref, w1_ref, w2_ref, x_ref, g_ref, ng_ref, sh_ref, sc_ref, o_ref, h_ref = refs
    d = x_ref.shape[1]
    y1 = y1_ref[...]
    y2 = y2_ref[...]
    g = g_ref[0]
    ss = None
    for c in range(d // tn):
        sl = slice(c * tn, (c + 1) * tn)
        acc = _dot(y1, w1_ref[:, sl]) + _dot(y2, w2_ref[:, sl])
        xn = x_ref[:, sl] + g[:, sl] * acc
        o_ref[:, sl] = xn
        part = jnp.sum(xn * xn, axis=-1, keepdims=True)
        ss = part if ss is None else ss + part
    inv = lax.rsqrt(ss / d + EPS)
    for c in range(d // tn):
        sl = slice(c * tn, (c + 1) * tn)
        normed = o_ref[:, sl] * inv * ng_ref[:, sl]
        if last:
            o_ref[:, sl] = normed
        else:
            h_ref[:, sl] = (normed * (1.0 + sc_ref[0][:, sl]) + sh_ref[0][:, sl]).astype(BF16)


def _out_proj(y1, y2, c1, c2, w_out, li, x, mod_l, next_g, mod_next, *, tm, tn, row_of):
    n_tok, d = x.shape
    kh = w_out.shape[1] // 2
    last = mod_next is None
    in_specs = [
        pl.BlockSpec((tm, kh), lambda i: (i, c1)),
        pl.BlockSpec((tm, kh), lambda i: (i, c2)),
        pl.BlockSpec((None, kh, d), lambda i: (li, 0, 0), pipeline_mode=pl.Buffered(1)),
        pl.BlockSpec((None, kh, d), lambda i: (li, 1, 0), pipeline_mode=pl.Buffered(1)),
        pl.BlockSpec((tm, d), lambda i: (i, 0)),
        pl.BlockSpec((1, 1, d), lambda i: (row_of(i * tm), 0, 2)),
        pl.BlockSpec((1, d), lambda i: (0, 0)),
    ]
    args = [y1, y2, w_out, w_out, x, mod_l, next_g]
    out_shape = [jax.ShapeDtypeStruct((n_tok, d), F32)]
    out_specs = [pl.BlockSpec((tm, d), lambda i: (i, 0))]
    if not last:
        in_specs += [pl.BlockSpec((1, 1, d), lambda i: (row_of(i * tm), 0, 0)),
                     pl.BlockSpec((1, 1, d), lambda i: (row_of(i * tm), 0, 1))]
        args += [mod_next, mod_next]
        out_shape.append(jax.ShapeDtypeStruct((n_tok, d), BF16))
        out_specs.append(pl.BlockSpec((tm, d), lambda i: (i, 0)))
    res = pl.pallas_call(
        functools.partial(_out_kernel, tn=tn, last=last),
        out_shape=tuple(out_shape),
        grid=(n_tok // tm,),
        in_specs=in_specs,
        out_specs=tuple(out_specs),
        compiler_params=_params("parallel"),
        name="out_proj",
    )(*args)
    return res[0] if last else res


def _v_ln_kernel(h_ref, wv_ref, lg_ref, lb_ref, vn_ref, vacc, mu_scr, *, nj, tnv, n):
    @pl.when(pl.program_id(0) == 0)
    def _():
        vacc[...] = jnp.zeros_like(vacc)
        mu_scr[...] = jnp.zeros_like(mu_scr)

    width = nj * tnv

    def prev_stats():
        mu = mu_scr[:, 0:1]
        sq = functools.reduce(
            jnp.add, [jnp.sum((vacc[t] - mu) * (vacc[t] - mu), axis=-1, keepdims=True) for t in range(nj)])
        return mu, lax.rsqrt(sq / width + EPS)

    def norm_prev(t, mu, inv):
        sl = slice(t * tnv, (t + 1) * tnv)
        vn_ref[:, sl] = ((vacc[t] - mu) * inv * lg_ref[:, sl] + lb_ref[:, sl]).astype(BF16)

    @pl.when(pl.program_id(0) < n)
    def _():
        mu, inv = prev_stats()
        h = h_ref[...]
        tot = None
        for t in range(nj):
            norm_prev(t, mu, inv)
            v_new = _dot(h, wv_ref[:, t * tnv:(t + 1) * tnv])
            vacc[t] = v_new
            part = jnp.sum(v_new, axis=-1, keepdims=True)
            tot = part if tot is None else tot + part
        mu_scr[...] = jnp.broadcast_to(tot / width, mu_scr.shape)

    @pl.when(pl.program_id(0) == n)
    def _():
        mu, inv = prev_stats()
        for t in range(nj):
            norm_prev(t, mu, inv)


def _v_ln(h, w_in, li, ln_g, ln_b, *, tm, tnv):
    n_tok, d = h.shape
    sgw = ln_g.shape[1]
    nj = sgw // tnv
    n = n_tok // tm
    return pl.pallas_call(
        functools.partial(_v_ln_kernel, nj=nj, tnv=tnv, n=n),
        out_shape=jax.ShapeDtypeStruct((n_tok, sgw), BF16),
        grid=(n + 1,),
        in_specs=[
            pl.BlockSpec((tm, d), lambda i: (jnp.minimum(i, n - 1), 0)),
            pl.BlockSpec((None, d, sgw), lambda i: (li, 0, 1), pipeline_mode=pl.Buffered(1)),
            pl.BlockSpec((1, sgw), lambda i: (0, 0)),
            pl.BlockSpec((1, sgw), lambda i: (0, 0)),
        ],
        out_specs=pl.BlockSpec((tm, sgw), lambda i: (jnp.maximum(i - 1, 0), 0)),
        scratch_shapes=[pltpu.VMEM((nj, tm, tnv), F32), pltpu.VMEM((tm, LANES), F32)],
        compiler_params=_params("arbitrary"),
        name="odd_v_ln",
    )(h, w_in, ln_g, ln_b)


def _sg_kernel(h_ref, wu_ref, wg_ref, vn_ref, ws_ref, bs_ref, y_ref, *, tm, chunk, gps):
    h = h_ref[...]
    gw = y_ref.shape[1] // gps
    for s in range(gps):
        cols = slice(s * gw, (s + 1) * gw)
        u = _dot(h, wu_ref[:, cols])
        g = _dot(h, wg_ref[:, cols])
        ws = ws_ref[s]
        bs = bs_ref[s]
        for c in range(tm // chunk):
            sl = slice(c * chunk, (c + 1) * chunk)
            vs = _dot(ws, vn_ref[sl, cols]) + bs
            y_ref[sl, cols] = (u[sl] * vs * _silu(g[sl])).astype(BF16)


def _spatial_gate(h, w_in, li, vn, w_s, b_s, *, tm, gps):
    n_tok, d = h.shape
    _, groups, chunk, _ = w_s.shape
    sgw = vn.shape[1]
    tn = gps * sgw // groups
    nj = groups // gps
    return pl.pallas_call(
        functools.partial(_sg_kernel, tm=tm, chunk=chunk, gps=gps),
        out_shape=jax.ShapeDtypeStruct((n_tok, sgw), BF16),
        grid=(n_tok // tm, nj),
        in_specs=[
            pl.BlockSpec((tm, d), lambda i, j: (i, 0)),
            pl.BlockSpec((None, d, tn), lambda i, j: (li, 0, j)),
            pl.BlockSpec((None, d, tn), lambda i, j: (li, 0, 2 * nj + j)),
            pl.BlockSpec((tm, tn), lambda i, j: (i, j)),
            pl.BlockSpec((None, gps, chunk, chunk), lambda i, j: (li, j, 0, 0)),
            pl.BlockSpec((None, gps, chunk, 1), lambda i, j: (li, j, 0, 0)),
        ],
        out_specs=pl.BlockSpec((tm, tn), lambda i, j: (i, j)),
        compiler_params=_params("parallel", "arbitrary"),
        name="odd_spatial_gate",
    )(h, w_in, w_in, vn, w_s, b_s)


def _rot_cols(w):
    shp = w.shape
    w4 = w.reshape(shp[:-1] + (2, 2, QK_ROPE // 4))
    return jnp.stack([-w4[..., 1, :], w4[..., 0, :]], axis=-2).reshape(shp)


def _rope_table(n):
    f32 = np.float32
    rows = n // GRID_W
    row = np.repeat(np.arange(rows, dtype=f32), GRID_W)
    col = np.tile(np.arange(GRID_W, dtype=f32), rows)
    n_freq = QK_ROPE // 4
    inv = np.power(f32(ROPE_BASE), -np.arange(n_freq, dtype=f32) / f32(n_freq)).astype(f32)
    ar = row[:, None] * inv
    ac = col[:, None] * inv
    ang = np.concatenate([ar, ar, ac, ac], axis=-1).astype(f32)
    cs = np.concatenate([np.cos(ang), np.sin(ang)], axis=-1).astype(f32)
    return jnp.asarray(cs), jnp.asarray(np.ascontiguousarray(cs.T))


def _tile(n, pref):
    return pref if n % pref == 0 else n


def _prep_weights(e_w_in, e_w_qb, e_w_kvb, e_w_out, o_w_in, o_w_s, o_w_out, cw, ql, kvl, heads):
    n_even = e_w_in.shape[0]
    o = 4 * cw + ql + kvl
    e_in = e_w_in[:, :, :o].astype(BF16)
    e_tail = e_w_in[:, :, o:].astype(BF16)
    w_kpe = e_tail[:, :, :QK_ROPE]
    n = np.arange(heads * HEAD_QK)
    hd, r = n // HEAD_QK, n % HEAD_QK
    j = r - QK_NOPE - QK_ROPE
    first = (j % (QK_ROPE // 2)) < QK_ROPE // 4
    src = np.where(j < 0, r, QK_NOPE + np.where(first, j + QK_ROPE // 4, j - QK_ROPE // 4))
    src = hd * (QK_NOPE + QK_ROPE) + src
    sign = np.where((j >= 0) & first, -1.0, 1.0).astype(np.float32)
    cols = lax.broadcasted_iota(jnp.int32, (heads * HEAD_QK, heads * (QK_NOPE + QK_ROPE)), 1)
    sel = jnp.where(cols == jnp.asarray(src, jnp.int32)[:, None], jnp.asarray(sign)[:, None], 0.0).astype(BF16)
    w_qbt = jnp.einsum("nc,lkc->lnk", sel, e_w_qb.astype(BF16), preferred_element_type=F32).astype(BF16)
    w_kvb = e_w_kvb.astype(BF16).reshape(n_even, kvl, heads, 2, V_HEAD)
    return dict(
        e_in=e_in,
        w_kpe2=jnp.concatenate([w_kpe, _rot_cols(w_kpe)], axis=-1),
        w_mg=e_tail[:, :, QK_ROPE:],
        w_qbt=w_qbt,
        w_k=w_kvb[:, :, :, 0, :].reshape(n_even, kvl, heads * QK_NOPE),
        w_vt=w_kvb[:, :, :, 1, :].reshape(n_even, kvl, heads * V_HEAD).transpose(0, 2, 1),
        e_out=e_w_out.astype(BF16),
        o_in=o_w_in.astype(BF16),
        w_s=o_w_s.astype(BF16),
        o_out=o_w_out.astype(BF16),
    )


def _trunk(x, *, batch, seq, cond_row, cs, cache, mod, norm_g, w, e_conv_w, e_q_norm_g, e_kv_norm_g,
           o_ln_g, o_ln_b, o_b_s, final_g, heads, emit):
    n_tok, d = x.shape
    depth = mod.shape[0]
    cw = e_conv_w.shape[2]
    ql = e_q_norm_g.shape[1]
    kvl = e_kv_norm_g.shape[1]
    assert seq & (seq - 1) == 0, "sequence length must be a power of two"
    cs, cst = (None, None) if cs is None else cs
    assert heads % HEAD_GROUP == 0 and (4 * cw) % ql == 0 and (4 * cw + ql) % kvl == 0 and cw == heads * V_HEAD
    tm = _tile(n_tok, 512)
    tml = _tile(n_tok, 1024)
    scale = float((QK_NOPE + QK_ROPE) ** -0.5 * math.log2(math.e))
    row_of = cond_row
    mods = [mod[l].reshape(mod.shape[1], 1, mod.shape[2]) for l in range(depth)]
    b_s = o_b_s[..., None]
    ckvs, kpes = [], []
    h = _first_h(x, mods[0], norm_g[0][None, :], tm=tm, row_of=row_of)
    for l in range(depth):
        i = l // 2
        if l % 2 == 0:
            y1 = _conv_branch(h, w["e_in"], i, e_conv_w[i], tm=tml, tn=_tile(cw, 512), seq=seq,
                              nsub=2 if _tile(cw, 512) % 512 == 0 else 1)
            q = _q_proj(h, w["e_in"], i, (4 * cw) // ql, e_q_norm_g[i][None, :], w["w_qbt"],
                        cst, tm=tml, seq=seq, heads=heads, scale=scale)
            kv = _kv_proj(h, w["e_in"], i, (4 * cw + ql) // kvl, e_kv_norm_g[i][None, :], w["w_kpe2"][i],
                          w["w_k"], w["w_vt"], cs, tm=tml, seq=seq, heads=heads, emit=emit)
            if emit:
                ckvs.append(kv[2])
                kpes.append(kv[3])
            gate = _gate(h, w["w_mg"][i], tm=tml, tn=_tile(w["w_mg"].shape[2], 512))
            if cache is None:
                y2 = _attention(q, kv[0], kv[1], gate, batch=batch, seq=seq, heads=heads)
            else:
                c_ckv, c_kpe = cache
                past = c_ckv.shape[2]
                kpe_pad = jnp.pad(c_kpe[:, i].reshape(batch * past, QK_ROPE),
                                  ((0, 0), (0, LANES - QK_ROPE))).astype(BF16)
                kc, vc = _cache_kv(c_ckv[:, i].reshape(batch * past, kvl), kpe_pad, w["w_k"], w["w_vt"], i,
                                   tm=_tile(batch * past, 512), heads=heads)
                y2 = _attention_pipelined(q, kc, vc, kv[0], kv[1], gate, batch=batch, seq=seq, past=past,
                                          heads=heads, tq=_tile(seq // 2, 512), nsub=2,
                                          kc_size=_tile(math.gcd(past, seq), 512))
            c2, w_out = 0, w["e_out"]
        else:
            vn = _v_ln(h, w["o_in"], i, o_ln_g[i][None, :], o_ln_b[i][None, :], tm=tm,
                       tnv=_tile(o_ln_g.shape[1], 512))
            groups = w["w_s"].shape[1]
            y1 = y2 = _spatial_gate(h, w["o_in"], i, vn, w["w_s"], b_s, tm=tml, gps=4 if groups % 4 == 0 else 1)
            c2, w_out = 1, w["o_out"]
        if l + 1 < depth:
            x, h = _out_proj(y1, y2, 0, c2, w_out, i, x, mods[l], norm_g[l + 1][None, :], mods[l + 1],
                             tm=tm, tn=_tile(d, 512), row_of=row_of)
        else:
            x = _out_proj(y1, y2, 0, c2, w_out, i, x, mods[l], final_g[None, :], None,
                          tm=tm, tn=_tile(d, 512), row_of=row_of)
    return x, ckvs, kpes


def kernel(x_prompt, x_sample, cache_ckv, cache_kpe, c, c_ctx, norm_g, w_ada, b_ada, e_w_in, e_conv_w,
           e_q_norm_g, e_w_qb, e_kv_norm_g, e_w_kvb, e_w_out, o_w_in, o_ln_g, o_ln_b, o_w_s, o_b_s, o_w_out,
           final_g):
    batch, seq, d = x_prompt.shape
    dec_batch, dec_seq, _ = x_sample.shape
    cw = e_conv_w.shape[2]
    ql = e_q_norm_g.shape[1]
    kvl = e_kv_norm_g.shape[1]
    heads = e_w_qb.shape[2] // (QK_NOPE + QK_ROPE)

    rows = -(-(dec_batch + 1) // SUBLANES) * SUBLANES
    cond = jnp.concatenate([c, c_ctx[None, :], jnp.zeros((rows - dec_batch - 1, d), F32)], axis=0)
    mod = _ada(cond, w_ada, b_ada)

    w = _prep_weights(e_w_in, e_w_qb, e_w_kvb, e_w_out, o_w_in, o_w_s, o_w_out, cw, ql, kvl, heads)
    shared = dict(mod=mod, norm_g=norm_g, w=w, e_conv_w=e_conv_w, e_q_norm_g=e_q_norm_g,
                  e_kv_norm_g=e_kv_norm_g, o_ln_g=o_ln_g, o_ln_b=o_ln_b, o_b_s=o_b_s,
                  final_g=final_g, heads=heads)

    y_prompt, ckvs, kpes = _trunk(x_prompt.reshape(batch * seq, d), batch=batch, seq=seq,
                                  cond_row=lambda r: dec_batch, cs=None, cache=None, emit=True, **shared)
    y_sample, _, _ = _trunk(x_sample.reshape(dec_batch * dec_seq, d), batch=dec_batch, seq=dec_seq,
                            cond_row=lambda r: r // dec_seq, cs=_rope_table(dec_seq),
                            cache=(cache_ckv, cache_kpe), emit=False, **shared)
    new_ckv = jnp.stack([t.reshape(batch, seq, kvl) for t in ckvs], axis=1)
    new_kpe = jnp.stack([t.reshape(batch, seq, QK_ROPE) for t in kpes], axis=1)
    return (y_prompt.reshape(batch, seq, d), y_sample.reshape(dec_batch, dec_seq, d), new_ckv, new_kpe)
```

```python
import functools
import math

import jax
import jax.numpy as jnp
import numpy as np
from jax import lax
from jax.experimental import pallas as pl
from jax.experimental.pallas import tpu as pltpu

F32 = jnp.float32
BF16 = jnp.bfloat16

EPS = 1e-6
QK_NOPE = 128
QK_ROPE = 64
V_HEAD = 128
HEAD_QK = 256
GRID_W = 64
ROPE_BASE = 10000.0

V7X_VMEM_BYTES = 64 * 1024 * 1024
VMEM_LIMIT = V7X_VMEM_BYTES - 8 * 1024 * 1024
SUBLANES = 8
PACKED_ROWS = 16
LANES = 128


def _silu(x):
    return x * jax.nn.sigmoid(x)


def _dot(a, b):
    return jnp.dot(a, b, preferred_element_type=F32)


def _rms(x, g):
    return x * lax.rsqrt(jnp.mean(x * x, axis=-1, keepdims=True) + EPS) * g


def _params(*sem):
    return pltpu.CompilerParams(dimension_semantics=sem, vmem_limit_bytes=VMEM_LIMIT)


def _ada_kernel(c_ref, w_ref, b_ref, o_ref):
    a = _silu(c_ref[...]).astype(BF16)
    o_ref[0] = _dot(a, w_ref[0].astype(BF16)) + b_ref[0]


def _ada(cond, w_ada, b_ada):
    depth, d, n = w_ada.shape
    rows = cond.shape[0]
    tn = 768 if n % 768 == 0 else n
    return pl.pallas_call(
        _ada_kernel,
        out_shape=jax.ShapeDtypeStruct((depth, rows, n), F32),
        grid=(depth, n // tn),
        in_specs=[
            pl.BlockSpec((rows, d), lambda l, j: (0, 0)),
            pl.BlockSpec((1, d, tn), lambda l, j: (l, 0, j)),
            pl.BlockSpec((1, 1, tn), lambda l, j: (l, 0, j)),
        ],
        out_specs=pl.BlockSpec((1, rows, tn), lambda l, j: (l, 0, j)),
        compiler_params=_params("parallel", "parallel"),
        name="ada_mod",
    )(cond, w_ada, b_ada.reshape(depth, 1, n))


def _norm_mod(x, g, sc, sh):
    return _rms(x, g) * (1.0 + sc) + sh


def _norm_mod_kernel(x_ref, ng_ref, sh_ref, sc_ref, h_ref):
    h_ref[...] = _norm_mod(x_ref[...], ng_ref[...], sc_ref[0], sh_ref[0]).astype(BF16)


def _first_h(x, mod_l, norm_g, *, tm, row_of):
    n_tok, d = x.shape
    return pl.pallas_call(
        _norm_mod_kernel,
        out_shape=jax.ShapeDtypeStruct((n_tok, d), BF16),
        grid=(n_tok // tm,),
        in_specs=[
            pl.BlockSpec((tm, d), lambda i: (i, 0)),
            pl.BlockSpec((1, d), lambda i: (0, 0)),
            pl.BlockSpec((1, 1, d), lambda i: (row_of(i * tm), 0, 0)),
            pl.BlockSpec((1, 1, d), lambda i: (row_of(i * tm), 0, 1)),
        ],
        out_specs=pl.BlockSpec((tm, d), lambda i: (i, 0)),
        compiler_params=_params("parallel"),
        name="norm_mod",
    )(x, norm_g, mod_l, mod_l)


def _conv_kernel(h_ref, hp_ref, hn_ref, wb_ref, wc_ref, wx_ref, wg_ref, cw_ref, y_ref, hx_scr, *, tm, seq, nsub):
    i = pl.program_id(0)
    j = pl.program_id(1)

    @pl.when(j == 0)
    def _():
        hx_scr[0:tm] = h_ref[...]
        hx_scr[tm:tm + PACKED_ROWS] = hp_ref[...]
        hx_scr[tm + PACKED_ROWS:tm + 2 * PACKED_ROWS] = hn_ref[...]

    he = hx_scr[...]
    hm = h_ref[...]
    ts = y_ref.shape[1] // nsub
    for s in range(nsub):
        cols = slice(s * ts, (s + 1) * ts)
        pe = _dot(he, wc_ref[:, cols]) * _dot(he, wx_ref[:, cols])
        p = pe[:tm]
        p_before = pe[tm + PACKED_ROWS - 1:tm + PACKED_ROWS]
        p_after = pe[tm + PACKED_ROWS:tm + PACKED_ROWS + 1]
        cb = _dot(hm, wb_ref[:, cols])
        cg = _dot(hm, wg_ref[:, cols])

        row = lax.broadcasted_iota(jnp.int32, p.shape, 0)
        pos = (i * tm + row) & (seq - 1)
        prev = jnp.where(row == 0, p_before, pltpu.roll(p, 1, 0))
        prev = jnp.where(pos == 0, 0.0, prev)
        nxt = jnp.where(row == tm - 1, p_after, pltpu.roll(p, tm - 1, 0))
        nxt = jnp.where(pos == seq - 1, 0.0, nxt)
        cw = cw_ref[:, cols]
        conv = prev * cw[0:1] + p * cw[1:2] + nxt * cw[2:3]
        y_ref[:, cols] = (cb * conv * _silu(cg)).astype(BF16)


def _conv_branch(h, w_in, li, conv_w, *, tm, tn, seq, nsub):
    n_tok, d = h.shape
    cw = conv_w.shape[1]
    nj = cw // tn
    n16 = n_tok // PACKED_ROWS
    t16 = tm // PACKED_ROWS
    wspec = lambda grp: pl.BlockSpec((None, d, tn), lambda i, j: (li, 0, grp * nj + j))
    return pl.pallas_call(
        functools.partial(_conv_kernel, tm=tm, seq=seq, nsub=nsub),
        out_shape=jax.ShapeDtypeStruct((n_tok, cw), BF16),
        grid=(n_tok // tm, nj),
        in_specs=[
            pl.BlockSpec((tm, d), lambda i, j: (i, 0)),
            pl.BlockSpec((PACKED_ROWS, d), lambda i, j: (jnp.maximum(i * t16 - 1, 0), 0)),
            pl.BlockSpec((PACKED_ROWS, d), lambda i, j: (jnp.minimum((i + 1) * t16, n16 - 1), 0)),
            wspec(0), wspec(1), wspec(2), wspec(3),
            pl.BlockSpec((3, tn), lambda i, j: (0, j)),
        ],
        out_specs=pl.BlockSpec((tm, tn), lambda i, j: (i, j)),
        scratch_shapes=[pltpu.VMEM((tm + 2 * PACKED_ROWS, d), BF16)],
        compiler_params=_params("parallel", "arbitrary"),
        name="even_conv",
    )(h, h, h, w_in, w_in, w_in, w_in, conv_w)


def _rope_pair(v, cs):
    t = v * cs
    return t + pltpu.roll(t, QK_ROPE, 1)


def _q_kernel(*refs, heads, rope, scale):
    if rope:
        h_ref, wqa_ref, qg_ref, wqbt_ref, cst_ref, qt_ref = refs
    else:
        h_ref, wqa_ref, qg_ref, wqbt_ref, qt_ref = refs
    qa = _dot(h_ref[...], wqa_ref[...])
    qnt = _rms(qa, qg_ref[...]).T.astype(BF16)
    zeros = jnp.zeros((HEAD_QK - QK_NOPE - QK_ROPE, qnt.shape[1]), BF16)
    for hd in range(heads):
        base = hd * HEAD_QK
        qt = _dot(wqbt_ref[base:base + HEAD_QK, :], qnt)
        pe = qt[QK_NOPE:]
        if rope:
            t = pe * cst_ref[...]
            pe = t[:QK_ROPE] + t[QK_ROPE:]
        else:
            pe = pe[:QK_ROPE]
        qt_ref[base:base + QK_NOPE, :] = (qt[:QK_NOPE] * scale).astype(BF16)
        qt_ref[base + QK_NOPE:base + QK_NOPE + QK_ROPE, :] = (pe * scale).astype(BF16)
        qt_ref[base + QK_NOPE + QK_ROPE:base + HEAD_QK, :] = zeros


def _q_proj(h, w_in, li, qa_block, q_norm_g, w_qbt, cst, *, tm, seq, heads, scale):
    n_tok, d = h.shape
    ql = q_norm_g.shape[1]
    rope = cst is not None
    in_specs = [
        pl.BlockSpec((tm, d), lambda i: (i, 0)),
        pl.BlockSpec((None, d, ql), lambda i: (li, 0, qa_block), pipeline_mode=pl.Buffered(1)),
        pl.BlockSpec((1, ql), lambda i: (0, 0)),
        pl.BlockSpec((None, heads * HEAD_QK, ql), lambda i: (li, 0, 0), pipeline_mode=pl.Buffered(1)),
    ]
    args = [h, w_in, q_norm_g, w_qbt]
    if rope:
        nb = seq // tm
        in_specs.append(pl.BlockSpec((LANES, tm), lambda i: (0, i % nb)))
        args.append(cst)
    return pl.pallas_call(
        functools.partial(_q_kernel, heads=heads, rope=rope, scale=scale),
        out_shape=jax.ShapeDtypeStruct((heads * HEAD_QK, n_tok), BF16),
        grid=(n_tok // tm,),
        in_specs=in_specs,
        out_specs=pl.BlockSpec((heads * HEAD_QK, tm), lambda i: (0, i)),
        compiler_params=_params("parallel"),
        name="mla_q_proj",
    )(*args)


HEAD_GROUP = 4


def _write_kv(ckv, kper, wk_ref, wvt_ref, k_ref, vt_ref, heads):
    cb = ckv.astype(BF16)
    cbt = ckv.T.astype(BF16)
    gw = HEAD_GROUP * V_HEAD
    for g in range(heads // HEAD_GROUP):
        kg = _dot(cb, wk_ref[:, g * gw:(g + 1) * gw])
        for j in range(HEAD_GROUP):
            base = (g * HEAD_GROUP + j) * HEAD_QK
            k_ref[:, base:base + QK_NOPE] = kg[:, j * QK_NOPE:(j + 1) * QK_NOPE].astype(BF16)
            k_ref[:, base + QK_NOPE:base + HEAD_QK] = kper
        vt_ref[g * gw:(g + 1) * gw, :] = _dot(wvt_ref[g * gw:(g + 1) * gw, :], cbt).astype(BF16)


def _kv_kernel(*refs, heads, rope, emit):
    refs = list(refs)
    h_ref, wckv_ref, kg_ref, wkpe_ref, wk_ref, wvt_ref = refs[:6]
    rest = refs[6:]
    cs_ref = rest.pop(0) if rope else None
    k_ref, vt_ref = rest[:2]
    h = h_ref[...]
    ckv = _rms(_dot(h, wckv_ref[...]), kg_ref[...])
    kp = _dot(h, wkpe_ref[...])
    if emit:
        ckv_ref, kpe_ref = rest[2:]
        ckv_ref[...] = ckv
        kpe_ref[...] = kp[:, :QK_ROPE]
    kk = _rope_pair(kp, cs_ref[...]) if rope else kp
    lane = lax.broadcasted_iota(jnp.int32, kk.shape, 1)
    kper = jnp.where(lane < QK_ROPE, kk, 0.0).astype(BF16)
    _write_kv(ckv, kper, wk_ref, wvt_ref, k_ref, vt_ref, heads)


def _kv_proj(h, w_in, li, ckv_block, kv_norm_g, w_kpe2, w_k, w_vt, cs, *, tm, seq, heads, emit):
    n_tok, d = h.shape
    kvl = kv_norm_g.shape[1]
    rope = cs is not None
    in_specs = [
        pl.BlockSpec((tm, d), lambda i: (i, 0)),
        pl.BlockSpec((None, d, kvl), lambda i: (li, 0, ckv_block), pipeline_mode=pl.Buffered(1)),
        pl.BlockSpec((1, kvl), lambda i: (0, 0)),
        pl.BlockSpec((d, LANES), lambda i: (0, 0)),
        pl.BlockSpec((None, kvl, heads * QK_NOPE), lambda i: (li, 0, 0), pipeline_mode=pl.Buffered(1)),
        pl.BlockSpec((None, heads * V_HEAD, kvl), lambda i: (li, 0, 0), pipeline_mode=pl.Buffered(1)),
    ]
    args = [h, w_in, kv_norm_g, w_kpe2, w_k, w_vt]
    if rope:
        nb = seq // tm
        in_specs.append(pl.BlockSpec((tm, LANES), lambda i: (i % nb, 0)))
        args.append(cs)
    out_shape = [jax.ShapeDtypeStruct((n_tok, heads * HEAD_QK), BF16),
                 jax.ShapeDtypeStruct((heads * V_HEAD, n_tok), BF16)]
    out_specs = [pl.BlockSpec((tm, heads * HEAD_QK), lambda i: (i, 0)),
                 pl.BlockSpec((heads * V_HEAD, tm), lambda i: (0, i))]
    if emit:
        out_shape += [jax.ShapeDtypeStruct((n_tok, kvl), F32), jax.ShapeDtypeStruct((n_tok, QK_ROPE), F32)]
        out_specs += [pl.BlockSpec((tm, kvl), lambda i: (i, 0)), pl.BlockSpec((tm, QK_ROPE), lambda i: (i, 0))]
    return pl.pallas_call(
        functools.partial(_kv_kernel, heads=heads, rope=rope, emit=emit),
        out_shape=tuple(out_shape),
        grid=(n_tok // tm,),
        in_specs=in_specs,
        out_specs=tuple(out_specs),
        compiler_params=_params("parallel"),
        name="mla_kv_proj",
    )(*args)


def _cache_kv_kernel(c_ref, kpe_ref, wk_ref, wvt_ref, k_ref, vt_ref, *, heads):
    _write_kv(c_ref[...], kpe_ref[...], wk_ref, wvt_ref, k_ref, vt_ref, heads)


def _cache_kv(ckv, kpe_pad, w_k, w_vt, li, *, tm, heads):
    n, kvl = ckv.shape
    return pl.pallas_call(
        functools.partial(_cache_kv_kernel, heads=heads),
        out_shape=(jax.ShapeDtypeStruct((n, heads * HEAD_QK), BF16),
                   jax.ShapeDtypeStruct((heads * V_HEAD, n), BF16)),
        grid=(n // tm,),
        in_specs=[
            pl.BlockSpec((tm, kvl), lambda i: (i, 0)),
            pl.BlockSpec((tm, LANES), lambda i: (i, 0)),
            pl.BlockSpec((None, kvl, heads * QK_NOPE), lambda i: (li, 0, 0), pipeline_mode=pl.Buffered(1)),
            pl.BlockSpec((None, heads * V_HEAD, kvl), lambda i: (li, 0, 0), pipeline_mode=pl.Buffered(1)),
        ],
        out_specs=(pl.BlockSpec((tm, heads * HEAD_QK), lambda i: (i, 0)),
                   pl.BlockSpec((heads * V_HEAD, tm), lambda i: (0, i))),
        compiler_params=_params("parallel"),
        name="mla_cache_kv",
    )(ckv, kpe_pad, w_k, w_vt)


def _gate_kernel(h_ref, w_ref, o_ref, *, tn):
    h = h_ref[...]
    for c in range(o_ref.shape[1] // tn):
        sl = slice(c * tn, (c + 1) * tn)
        o_ref[:, sl] = _silu(_dot(h, w_ref[:, sl])).astype(BF16)


def _gate(h, w, *, tm, tn):
    n_tok, d = h.shape
    n = w.shape[1]
    return pl.pallas_call(
        functools.partial(_gate_kernel, tn=tn),
        out_shape=jax.ShapeDtypeStruct((n_tok, n), BF16),
        grid=(n_tok // tm,),
        in_specs=[pl.BlockSpec((tm, d), lambda i: (i, 0)), pl.BlockSpec((d, n), lambda i: (0, 0))],
        out_specs=pl.BlockSpec((tm, n), lambda i: (i, 0)),
        compiler_params=_params("parallel"),
        name="mla_gate",
    )(h, w)


def _fold8(x, op):
    return op(x.reshape(x.shape[0] // SUBLANES, SUBLANES, x.shape[1]), axis=0)


def _attn_kernel(qt_ref, k_ref, vt_ref, gate_ref, o_ref, *, heads):
    for hd in range(heads):
        qk = slice(hd * HEAD_QK, (hd + 1) * HEAD_QK)
        cols = slice(hd * V_HEAD, (hd + 1) * V_HEAD)
        st = _dot(k_ref[:, qk], qt_ref[qk, :])
        pt = jnp.exp2(st - jnp.max(st, axis=0, keepdims=True))
        ot = _dot(vt_ref[cols, :], pt.astype(BF16)) / jnp.sum(pt, axis=0, keepdims=True)
        o_ref[:, cols] = (ot.T * gate_ref[:, cols].astype(F32)).astype(BF16)


def _attention(qt, k, vt, gate, *, batch, seq, heads):
    n_tok = k.shape[0]
    return pl.pallas_call(
        functools.partial(_attn_kernel, heads=heads),
        out_shape=jax.ShapeDtypeStruct((n_tok, heads * V_HEAD), BF16),
        grid=(batch,),
        in_specs=[
            pl.BlockSpec((heads * HEAD_QK, seq), lambda b: (0, b)),
            pl.BlockSpec((seq, heads * HEAD_QK), lambda b: (b, 0)),
            pl.BlockSpec((heads * V_HEAD, seq), lambda b: (0, b)),
            pl.BlockSpec((seq, heads * V_HEAD), lambda b: (b, 0)),
        ],
        out_specs=pl.BlockSpec((seq, heads * V_HEAD), lambda b: (b, 0)),
        compiler_params=_params("parallel"),
        name="mla_attention",
    )(qt, k, vt, gate)


def _attn_pipe_kernel(qt_ref, kc_ref, ko_ref, vtc_ref, vto_ref, gate_ref, o_ref, s_scr, m_scr, *, kc, nsub):
    @pl.when(jnp.logical_and(pl.program_id(0) == 0, pl.program_id(1) == 0))
    def _():
        s_scr[...] = jnp.zeros_like(s_scr)
        m_scr[...] = jnp.zeros_like(m_scr)

    tq = qt_ref.shape[1] // nsub
    half = tq // 2
    for u in range(nsub):
        qcols = slice(u * tq, (u + 1) * tq)
        qa = qt_ref[:, u * tq:u * tq + half]
        qb = qt_ref[:, u * tq + half:(u + 1) * tq]
        m_old = m_scr[u][0:1]
        acc = l_acc = m_acc = None
        off = 0
        for k_ref, vt_ref in ((kc_ref, vtc_ref), (ko_ref, vto_ref)):
            for c in range(k_ref.shape[0] // kc):
                rows = slice(c * kc, (c + 1) * kc)
                srows = slice(off, off + kc)
                pt = jnp.exp2(s_scr[u, srows, :] - m_old)
                lsum = _fold8(pt, jnp.sum)
                l_acc = lsum if l_acc is None else l_acc + lsum
                pv = _dot(vt_ref[:, rows], pt.astype(BF16))
                acc = pv if acc is None else acc + pv
                k_c = k_ref[rows, :]
                sa = _dot(k_c, qa)
                sb = _dot(k_c, qb)
                s_scr[u, srows, 0:half] = sa
                s_scr[u, srows, half:tq] = sb
                mx = jnp.concatenate([_fold8(sa, jnp.max), _fold8(sb, jnp.max)], axis=1)
                m_acc = mx if m_acc is None else jnp.maximum(m_acc, mx)
                off += kc
        m_scr[u] = jnp.broadcast_to(jnp.max(m_acc, axis=0, keepdims=True), (SUBLANES, tq))
        ot = acc / jnp.sum(l_acc, axis=0, keepdims=True)
        o_ref[qcols, :] = (ot.T * gate_ref[qcols, :].astype(F32)).astype(BF16)


def _attention_pipelined(qt, kc, vtc, ko, vto, gate, *, batch, seq, past, heads, tq, nsub, kc_size):
    n_tok = ko.shape[0]
    tsub, tq = tq, tq * nsub
    nq = seq // tq
    jobs = heads * nq
    jq = lambda g: jnp.minimum(g, jobs - 1)
    jp = lambda g: jnp.maximum(g - 1, 0)
    return pl.pallas_call(
        functools.partial(_attn_pipe_kernel, kc=kc_size, nsub=nsub),
        out_shape=jax.ShapeDtypeStruct((n_tok, heads * V_HEAD), BF16),
        grid=(batch, jobs + 1),
        in_specs=[
            pl.BlockSpec((HEAD_QK, tq), lambda b, g: (jq(g) // nq, b * nq + jq(g) % nq)),
            pl.BlockSpec((past, HEAD_QK), lambda b, g: (b, jq(g) // nq)),
            pl.BlockSpec((seq, HEAD_QK), lambda b, g: (b, jq(g) // nq)),
            pl.BlockSpec((V_HEAD, past), lambda b, g: (jp(g) // nq, b)),
            pl.BlockSpec((V_HEAD, seq), lambda b, g: (jp(g) // nq, b)),
            pl.BlockSpec((tq, V_HEAD), lambda b, g: (b * nq + jp(g) % nq, jp(g) // nq)),
        ],
        out_specs=pl.BlockSpec((tq, V_HEAD), lambda b, g: (b * nq + jp(g) % nq, jp(g) // nq)),
        scratch_shapes=[pltpu.VMEM((nsub, past + seq, tsub), F32), pltpu.VMEM((nsub, SUBLANES, tsub), F32)],
        compiler_params=_params("arbitrary", "arbitrary"),
        name="mla_attention_pipelined",
    )(qt, kc, ko, vtc, vto, gate)


def _out_kernel(*refs, tn, last):
    if last:
        y1_ref, y2_ref, w1_ref, w2_ref, x_ref, g_ref, ng_ref, o_ref = refs
    else:
        y1_ref, y2_ref, w1_ref, w2_ref, x_ref, g_ref, ng_ref, sh_ref, sc_ref, o_ref, h_ref = refs
    d = x_ref.shape[1]
    y1 = y1_ref[...]
    y2 = y2_ref[...]
    g = g_ref[0]
    ss = None
    for c in range(d // tn):
        sl = slice(c * tn, (c + 1) * tn)
        acc = _dot(y1, w1_ref[:, sl]) + _dot(y2, w2_ref[:, sl])
        xn = x_ref[:, sl] + g[:, sl] * acc
        o_ref[:, sl] = xn
        part = jnp.sum(xn * xn, axis=-1, keepdims=True)
        ss = part if ss is None else ss + part
    inv = lax.rsqrt(ss / d + EPS)
    for c in range(d // tn):
        sl = slice(c * tn, (c + 1) * tn)
        normed = o_ref[:, sl] * inv * ng_ref[:, sl]
        if last:
            o_ref[:, sl] = normed
        else:
            h_ref[:, sl] = (normed * (1.0 + sc_ref[0][:, sl]) + sh_ref[0][:, sl]).astype(BF16)


def _out_proj(y1, y2, c1, c2, w_out, li, x, mod_l, next_g, mod_next, *, tm, tn, row_of):
    n_tok, d = x.shape
    kh = w_out.shape[1] // 2
    last = mod_next is None
    in_specs = [
        pl.BlockSpec((tm, kh), lambda i: (i, c1)),
        pl.BlockSpec((tm, kh), lambda i: (i, c2)),
        pl.BlockSpec((None, kh, d), lambda i: (li, 0, 0), pipeline_mode=pl.Buffered(1)),
        pl.BlockSpec((None, kh, d), lambda i: (li, 1, 0), pipeline_mode=pl.Buffered(1)),
        pl.BlockSpec((tm, d), lambda i: (i, 0)),
        pl.BlockSpec((1, 1, d), lambda i: (row_of(i * tm), 0, 2)),
        pl.BlockSpec((1, d), lambda i: (0, 0)),
    ]
    args = [y1, y2, w_out, w_out, x, mod_l, next_g]
    out_shape = [jax.ShapeDtypeStruct((n_tok, d), F32)]
    out_specs = [pl.BlockSpec((tm, d), lambda i: (i, 0))]
    if not last:
        in_specs += [pl.BlockSpec((1, 1, d), lambda i: (row_of(i * tm), 0, 0)),
                     pl.BlockSpec((1, 1, d), lambda i: (row_of(i * tm), 0, 1))]
        args += [mod_next, mod_next]
        out_shape.append(jax.ShapeDtypeStruct((n_tok, d), BF16))
        out_specs.append(pl.BlockSpec((tm, d), lambda i: (i, 0)))
    res = pl.pallas_call(
        functools.partial(_out_kernel, tn=tn, last=last),
        out_shape=tuple(out_shape),
        grid=(n_tok // tm,),
        in_specs=in_specs,
        out_specs=tuple(out_specs),
        compiler_params=_params("parallel"),
        name="out_proj",
    )(*args)
    return res[0] if last else res


def _v_ln_kernel(h_ref, wv_ref, lg_ref, lb_ref, vn_ref, vacc, mu_scr, *, nj, tnv, n):
    @pl.when(pl.program_id(0) == 0)
    def _():
        vacc[...] = jnp.zeros_like(vacc)
        mu_scr[...] = jnp.zeros_like(mu_scr)

    width = nj * tnv

    def prev_stats():
        mu = mu_scr[:, 0:1]
        sq = functools.reduce(
            jnp.add, [jnp.sum((vacc[t] - mu) * (vacc[t] - mu), axis=-1, keepdims=True) for t in range(nj)])
        return mu, lax.rsqrt(sq / width + EPS)

    def norm_prev(t, mu, inv):
        sl = slice(t * tnv, (t + 1) * tnv)
        vn_ref[:, sl] = ((vacc[t] - mu) * inv * lg_ref[:, sl] + lb_ref[:, sl]).astype(BF16)

    @pl.when(pl.program_id(0) < n)
    def _():
        mu, inv = prev_stats()
        h = h_ref[...]
        tot = None
        for t in range(nj):
            norm_prev(t, mu, inv)
            v_new = _dot(h, wv_ref[:, t * tnv:(t + 1) * tnv])
            vacc[t] = v_new
            part = jnp.sum(v_new, axis=-1, keepdims=True)
            tot = part if tot is None else tot + part
        mu_scr[...] = jnp.broadcast_to(tot / width, mu_scr.shape)

    @pl.when(pl.program_id(0) == n)
    def _():
        mu, inv = prev_stats()
        for t in range(nj):
            norm_prev(t, mu, inv)


def _v_ln(h, w_in, li, ln_g, ln_b, *, tm, tnv):
    n_tok, d = h.shape
    sgw = ln_g.shape[1]
    nj = sgw // tnv
    n = n_tok // tm
    return pl.pallas_call(
        functools.partial(_v_ln_kernel, nj=nj, tnv=tnv, n=n),
        out_shape=jax.ShapeDtypeStruct((n_tok, sgw), BF16),
        grid=(n + 1,),
        in_specs=[
            pl.BlockSpec((tm, d), lambda i: (jnp.minimum(i, n - 1), 0)),
            pl.BlockSpec((None, d, sgw), lambda i: (li, 0, 1), pipeline_mode=pl.Buffered(1)),
            pl.BlockSpec((1, sgw), lambda i: (0, 0)),
            pl.BlockSpec((1, sgw), lambda i: (0, 0)),
        ],
        out_specs=pl.BlockSpec((tm, sgw), lambda i: (jnp.maximum(i - 1, 0), 0)),
        scratch_shapes=[pltpu.VMEM((nj, tm, tnv), F32), pltpu.VMEM((tm, LANES), F32)],
        compiler_params=_params("arbitrary"),
        name="odd_v_ln",
    )(h, w_in, ln_g, ln_b)


def _sg_kernel(h_ref, wu_ref, wg_ref, vn_ref, ws_ref, bs_ref, y_ref, *, tm, chunk, gps):
    h = h_ref[...]
    gw = y_ref.shape[1] // gps
    for s in range(gps):
        cols = slice(s * gw, (s + 1) * gw)
        u = _dot(h, wu_ref[:, cols])
        g = _dot(h, wg_ref[:, cols])
        ws = ws_ref[s]
        bs = bs_ref[s]
        for c in range(tm // chunk):
            sl = slice(c * chunk, (c + 1) * chunk)
            vs = _dot(ws, vn_ref[sl, cols]) + bs
            y_ref[sl, cols] = (u[sl] * vs * _silu(g[sl])).astype(BF16)


def _spatial_gate(h, w_in, li, vn, w_s, b_s, *, tm, gps):
    n_tok, d = h.shape
    _, groups, chunk, _ = w_s.shape
    sgw = vn.shape[1]
    tn = gps * sgw // groups
    nj = groups // gps
    return pl.pallas_call(
        functools.partial(_sg_kernel, tm=tm, chunk=chunk, gps=gps),
        out_shape=jax.ShapeDtypeStruct((n_tok, sgw), BF16),
        grid=(n_tok // tm, nj),
        in_specs=[
            pl.BlockSpec((tm, d), lambda i, j: (i, 0)),
            pl.BlockSpec((None, d, tn), lambda i, j: (li, 0, j)),
            pl.BlockSpec((None, d, tn), lambda i, j: (li, 0, 2 * nj + j)),
            pl.BlockSpec((tm, tn), lambda i, j: (i, j)),
            pl.BlockSpec((None, gps, chunk, chunk), lambda i, j: (li, j, 0, 0)),
            pl.BlockSpec((None, gps, chunk, 1), lambda i, j: (li, j, 0, 0)),
        ],
        out_specs=pl.BlockSpec((tm, tn), lambda i, j: (i, j)),
        compiler_params=_params("parallel", "arbitrary"),
        name="odd_spatial_gate",
    )(h, w_in, w_in, vn, w_s, b_s)


def _rot_cols(w):
    shp = w.shape
    w4 = w.reshape(shp[:-1] + (2, 2, QK_ROPE // 4))
    return jnp.stack([-w4[..., 1, :], w4[..., 0, :]], axis=-2).reshape(shp)


def _rope_table(n):
    f32 = np.float32
    rows = n // GRID_W
    row = np.repeat(np.arange(rows, dtype=f32), GRID_W)
    col = np.tile(np.arange(GRID_W, dtype=f32), rows)
    n_freq = QK_ROPE // 4
    inv = np.power(f32(ROPE_BASE), -np.arange(n_freq, dtype=f32) / f32(n_freq)).astype(f32)
    ar = row[:, None] * inv
    ac = col[:, None] * inv
    ang = np.concatenate([ar, ar, ac, ac], axis=-1).astype(f32)
    cs = np.concatenate([np.cos(ang), np.sin(ang)], axis=-1).astype(f32)
    return jnp.asarray(cs), jnp.asarray(np.ascontiguousarray(cs.T))


def _tile(n, pref):
    return pref if n % pref == 0 else n


def _prep_weights(e_w_in, e_w_qb, e_w_kvb, e_w_out, o_w_in, o_w_s, o_w_out, cw, ql, kvl, heads):
    n_even = e_w_in.shape[0]
    o = 4 * cw + ql + kvl
    e_in = e_w_in.astype(BF16)
    w_kpe = e_in[:, :, o:o + QK_ROPE]
    n = np.arange(heads * HEAD_QK)
    hd, r = n // HEAD_QK, n % HEAD_QK
    j = r - QK_NOPE - QK_ROPE
    first = (j % (QK_ROPE // 2)) < QK_ROPE // 4
    src = np.where(j < 0, r, QK_NOPE + np.where(first, j + QK_ROPE // 4, j - QK_ROPE // 4))
    src = hd * (QK_NOPE + QK_ROPE) + src
    sign = np.where((j >= 0) & first, -1.0, 1.0).astype(np.float32)
    cols = lax.broadcasted_iota(jnp.int32, (heads * HEAD_QK, heads * (QK_NOPE + QK_ROPE)), 1)
    sel = jnp.where(cols == jnp.asarray(src, jnp.int32)[:, None], jnp.asarray(sign)[:, None], 0.0).astype(BF16)
    w_qbt = jnp.einsum("nc,lkc->lnk", sel, e_w_qb.astype(BF16), preferred_element_type=F32).astype(BF16)
    w_kvb = e_w_kvb.astype(BF16).reshape(n_even, kvl, heads, 2, V_HEAD)
    return dict(
        e_in=e_in,
        w_kpe2=jnp.concatenate([w_kpe, _rot_cols(w_kpe)], axis=-1),
        w_mg=e_in[:, :, o + QK_ROPE:],
        w_qbt=w_qbt,
        w_k=w_kvb[:, :, :, 0, :].reshape(n_even, kvl, heads * QK_NOPE),
        w_vt=w_kvb[:, :, :, 1, :].reshape(n_even, kvl, heads * V_HEAD).transpose(0, 2, 1),
        e_out=e_w_out.astype(BF16),
        o_in=o_w_in.astype(BF16),
        w_s=o_w_s.astype(BF16),
        o_out=o_w_out.astype(BF16),
    )


def _trunk(x, *, batch, seq, cond_row, cs, cache, mod, norm_g, w, e_conv_w, e_q_norm_g, e_kv_norm_g,
           o_ln_g, o_ln_b, o_b_s, final_g, heads, emit):
    n_tok, d = x.shape
    depth = mod.shape[0]
    cw = e_conv_w.shape[2]
    ql = e_q_norm_g.shape[1]
    kvl = e_kv_norm_g.shape[1]
    assert seq & (seq - 1) == 0, "sequence length must be a power of two"
    cs, cst = (None, None) if cs is None else cs
    assert heads % HEAD_GROUP == 0 and (4 * cw) % ql == 0 and (4 * cw + ql) % kvl == 0 and cw == heads * V_HEAD
    tm = _tile(n_tok, 512)
    tml = _tile(n_tok, 1024)
    scale = float((QK_NOPE + QK_ROPE) ** -0.5 * math.log2(math.e))
    row_of = cond_row
    mods = [mod[l].reshape(mod.shape[1], 1, mod.shape[2]) for l in range(depth)]
    b_s = o_b_s[..., None]
    ckvs, kpes = [], []
    h = _first_h(x, mods[0], norm_g[0][None, :], tm=tm, row_of=row_of)
    for l in range(depth):
        i = l // 2
        if l % 2 == 0:
            y1 = _conv_branch(h, w["e_in"], i, e_conv_w[i], tm=tml, tn=_tile(cw, 512), seq=seq,
                              nsub=2 if _tile(cw, 512) % 512 == 0 else 1)
            q = _q_proj(h, w["e_in"], i, (4 * cw) // ql, e_q_norm_g[i][None, :], w["w_qbt"],
                        cst, tm=tml, seq=seq, heads=heads, scale=scale)
            kv = _kv_proj(h, w["e_in"], i, (4 * cw + ql) // kvl, e_kv_norm_g[i][None, :], w["w_kpe2"][i],
                          w["w_k"], w["w_vt"], cs, tm=tml, seq=seq, heads=heads, emit=emit)
            if emit:
                ckvs.append(kv[2])
                kpes.append(kv[3])
            gate = _gate(h, w["w_mg"][i], tm=tml, tn=_tile(w["w_mg"].shape[2], 512))
            if cache is None:
                y2 = _attention(q, kv[0], kv[1], gate, batch=batch, seq=seq, heads=heads)
            else:
                c_ckv, c_kpe = cache
                past = c_ckv.shape[2]
                kpe_pad = jnp.pad(c_kpe[:, i].reshape(batch * past, QK_ROPE),
                                  ((0, 0), (0, LANES - QK_ROPE))).astype(BF16)
                kc, vc = _cache_kv(c_ckv[:, i].reshape(batch * past, kvl), kpe_pad, w["w_k"], w["w_vt"], i,
                                   tm=_tile(batch * past, 512), heads=heads)
                y2 = _attention_pipelined(q, kc, vc, kv[0], kv[1], gate, batch=batch, seq=seq, past=past,
                                          heads=heads, tq=_tile(seq // 2, 512), nsub=2,
                                          kc_size=_tile(math.gcd(past, seq), 256))
            c2, w_out = 0, w["e_out"]
        else:
            vn = _v_ln(h, w["o_in"], i, o_ln_g[i][None, :], o_ln_b[i][None, :], tm=tm,
                       tnv=_tile(o_ln_g.shape[1], 256))
            groups = w["w_s"].shape[1]
            y1 = y2 = _spatial_gate(h, w["o_in"], i, vn, w["w_s"], b_s, tm=tml, gps=4 if groups % 4 == 0 else 1)
            c2, w_out = 1, w["o_out"]
        if l + 1 < depth:
            x, h = _out_proj(y1, y2, 0, c2, w_out, i, x, mods[l], norm_g[l + 1][None, :], mods[l + 1],
                             tm=tm, tn=_tile(d, 512), row_of=row_of)
        else:
            x = _out_proj(y1, y2, 0, c2, w_out, i, x, mods[l], final_g[None, :], None,
                          tm=tm, tn=_tile(d, 512), row_of=row_of)
    return x, ckvs, kpes


def kernel(x_prompt, x_sample, cache_ckv, cache_kpe, c, c_ctx, norm_g, w_ada, b_ada, e_w_in, e_conv_w,
           e_q_norm_g, e_w_qb, e_kv_norm_g, e_w_kvb, e_w_out, o_w_in, o_ln_g, o_ln_b, o_w_s, o_b_s, o_w_out,
           final_g):
    batch, seq, d = x_prompt.shape
    dec_batch, dec_seq, _ = x_sample.shape
    cw = e_conv_w.shape[2]
    ql = e_q_norm_g.shape[1]
    kvl = e_kv_norm_g.shape[1]
    heads = e_w_qb.shape[2] // (QK_NOPE + QK_ROPE)

    rows = -(-(dec_batch + 1) // SUBLANES) * SUBLANES
    cond = jnp.concatenate([c, c_ctx[None, :], jnp.zeros((rows - dec_batch - 1, d), F32)], axis=0)
    mod = _ada(cond, w_ada, b_ada)

    w = _prep_weights(e_w_in, e_w_qb, e_w_kvb, e_w_out, o_w_in, o_w_s, o_w_out, cw, ql, kvl, heads)
    shared = dict(mod=mod, norm_g=norm_g, w=w, e_conv_w=e_conv_w, e_q_norm_g=e_q_norm_g,
                  e_kv_norm_g=e_kv_norm_g, o_ln_g=o_ln_g, o_ln_b=o_ln_b, o_b_s=o_b_s,
                  final_g=final_g, heads=heads)

    y_prompt, ckvs, kpes = _trunk(x_prompt.reshape(batch * seq, d), batch=batch, seq=seq,
                                  cond_row=lambda r: dec_batch, cs=None, cache=None, emit=True, **shared)
    y_sample, _, _ = _trunk(x_sample.reshape(dec_batch * dec_seq, d), batch=dec_batch, seq=dec_seq,
                            cond_row=lambda r: r // dec_seq, cs=_rope_table(dec_seq),
                            cache=(cache_ckv, cache_kpe), emit=False, **shared)
    new_ckv = jnp.stack([t.reshape(batch, seq, kvl) for t in ckvs], axis=1)
    new_kpe = jnp.stack([t.reshape(batch, seq, QK_ROPE) for t in kpes], axis=1)
    return (y_prompt.reshape(batch, seq, d), y_sample.reshape(dec_batch, dec_seq, d), new_ckv, new_kpe)
```

```python
import functools
import math

import jax
import jax.numpy as jnp
import numpy as np
from jax import lax
from jax.experimental import pallas as pl
from jax.experimental.pallas import tpu as pltpu

F32 = jnp.float32
BF16 = jnp.bfloat16

EPS = 1e-6
QK_NOPE = 128
QK_ROPE = 64
V_HEAD = 128
HEAD_QK = 256
GRID_W = 64
ROPE_BASE = 10000.0

V7X_VMEM_BYTES = 64 * 1024 * 1024
VMEM_LIMIT = V7X_VMEM_BYTES - 8 * 1024 * 1024
SUBLANES = 8
PACKED_ROWS = 16
LANES = 128


def _silu(x):
    return x * jax.nn.sigmoid(x)


def _dot(a, b):
    return jnp.dot(a, b, preferred_element_type=F32)


def _rms(x, g):
    return x * lax.rsqrt(jnp.mean(x * x, axis=-1, keepdims=True) + EPS) * g


def _params(*sem):
    return pltpu.CompilerParams(dimension_semantics=sem, vmem_limit_bytes=VMEM_LIMIT)


def _ada_kernel(c_ref, w_ref, b_ref, o_ref):
    a = _silu(c_ref[...]).astype(BF16)
    o_ref[0] = _dot(a, w_ref[0].astype(BF16)) + b_ref[0]


def _ada(cond, w_ada, b_ada):
    depth, d, n = w_ada.shape
    rows = cond.shape[0]
    tn = 768 if n % 768 == 0 else n
    return pl.pallas_call(
        _ada_kernel,
        out_shape=jax.ShapeDtypeStruct((depth, rows, n), F32),
        grid=(depth, n // tn),
        in_specs=[
            pl.BlockSpec((rows, d), lambda l, j: (0, 0)),
            pl.BlockSpec((1, d, tn), lambda l, j: (l, 0, j)),
            pl.BlockSpec((1, 1, tn), lambda l, j: (l, 0, j)),
        ],
        out_specs=pl.BlockSpec((1, rows, tn), lambda l, j: (l, 0, j)),
        compiler_params=_params("parallel", "parallel"),
        name="ada_mod",
    )(cond, w_ada, b_ada.reshape(depth, 1, n))


def _norm_mod(x, g, sc, sh):
    return _rms(x, g) * (1.0 + sc) + sh


def _norm_mod_kernel(x_ref, ng_ref, sh_ref, sc_ref, h_ref):
    h_ref[...] = _norm_mod(x_ref[...], ng_ref[...], sc_ref[0], sh_ref[0]).astype(BF16)


def _first_h(x, mod_l, norm_g, *, tm, row_of):
    n_tok, d = x.shape
    return pl.pallas_call(
        _norm_mod_kernel,
        out_shape=jax.ShapeDtypeStruct((n_tok, d), BF16),
        grid=(n_tok // tm,),
        in_specs=[
            pl.BlockSpec((tm, d), lambda i: (i, 0)),
            pl.BlockSpec((1, d), lambda i: (0, 0)),
            pl.BlockSpec((1, 1, d), lambda i: (row_of(i * tm), 0, 0)),
            pl.BlockSpec((1, 1, d), lambda i: (row_of(i * tm), 0, 1)),
        ],
        out_specs=pl.BlockSpec((tm, d), lambda i: (i, 0)),
        compiler_params=_params("parallel"),
        name="norm_mod",
    )(x, norm_g, mod_l, mod_l)


def _conv_kernel(h_ref, hp_ref, hn_ref, wb_ref, wc_ref, wx_ref, wg_ref, cw_ref, y_ref, hx_scr, *, tm, seq, nsub):
    i = pl.program_id(0)
    j = pl.program_id(1)

    @pl.when(j == 0)
    def _():
        hx_scr[0:tm] = h_ref[...]
        hx_scr[tm:tm + PACKED_ROWS] = hp_ref[...]
        hx_scr[tm + PACKED_ROWS:tm + 2 * PACKED_ROWS] = hn_ref[...]

    he = hx_scr[...]
    hm = h_ref[...]
    ts = y_ref.shape[1] // nsub
    for s in range(nsub):
        cols = slice(s * ts, (s + 1) * ts)
        pe = _dot(he, wc_ref[:, cols]) * _dot(he, wx_ref[:, cols])
        p = pe[:tm]
        p_before = pe[tm + PACKED_ROWS - 1:tm + PACKED_ROWS]
        p_after = pe[tm + PACKED_ROWS:tm + PACKED_ROWS + 1]
        cb = _dot(hm, wb_ref[:, cols])
        cg = _dot(hm, wg_ref[:, cols])

        row = lax.broadcasted_iota(jnp.int32, p.shape, 0)
        pos = (i * tm + row) & (seq - 1)
        prev = jnp.where(row == 0, p_before, pltpu.roll(p, 1, 0))
        prev = jnp.where(pos == 0, 0.0, prev)
        nxt = jnp.where(row == tm - 1, p_after, pltpu.roll(p, tm - 1, 0))
        nxt = jnp.where(pos == seq - 1, 0.0, nxt)
        cw = cw_ref[:, cols]
        conv = prev * cw[0:1] + p * cw[1:2] + nxt * cw[2:3]
        y_ref[:, cols] = (cb * conv * _silu(cg)).astype(BF16)


def _conv_branch(h, w_in, li, conv_w, *, tm, tn, seq, nsub):
    n_tok, d = h.shape
    cw = conv_w.shape[1]
    nj = cw // tn
    n16 = n_tok // PACKED_ROWS
    t16 = tm // PACKED_ROWS
    wspec = lambda grp: pl.BlockSpec((None, d, tn), lambda i, j: (li, 0, grp * nj + j))
    return pl.pallas_call(
        functools.partial(_conv_kernel, tm=tm, seq=seq, nsub=nsub),
        out_shape=jax.ShapeDtypeStruct((n_tok, cw), BF16),
        grid=(n_tok // tm, nj),
        in_specs=[
            pl.BlockSpec((tm, d), lambda i, j: (i, 0)),
            pl.BlockSpec((PACKED_ROWS, d), lambda i, j: (jnp.maximum(i * t16 - 1, 0), 0)),
            pl.BlockSpec((PACKED_ROWS, d), lambda i, j: (jnp.minimum((i + 1) * t16, n16 - 1), 0)),
            wspec(0), wspec(1), wspec(2), wspec(3),
            pl.BlockSpec((3, tn), lambda i, j: (0, j)),
        ],
        out_specs=pl.BlockSpec((tm, tn), lambda i, j: (i, j)),
        scratch_shapes=[pltpu.VMEM((tm + 2 * PACKED_ROWS, d), BF16)],
        compiler_params=_params("parallel", "arbitrary"),
        name="even_conv",
    )(h, h, h, w_in, w_in, w_in, w_in, conv_w)


def _rope_pair(v, cs):
    t = v * cs
    return t + pltpu.roll(t, QK_ROPE, 1)


def _q_kernel(*refs, heads, rope, scale):
    if rope:
        h_ref, wqa_ref, qg_ref, wqbt_ref, cst_ref, qt_ref = refs
    else:
        h_ref, wqa_ref, qg_ref, wqbt_ref, qt_ref = refs
    qa = _dot(h_ref[...], wqa_ref[...])
    qnt = _rms(qa, qg_ref[...]).T.astype(BF16)
    zeros = jnp.zeros((HEAD_QK - QK_NOPE - QK_ROPE, qnt.shape[1]), BF16)
    for hd in range(heads):
        base = hd * HEAD_QK
        qt = _dot(wqbt_ref[base:base + HEAD_QK, :], qnt)
        pe = qt[QK_NOPE:]
        if rope:
            t = pe * cst_ref[...]
            pe = t[:QK_ROPE] + t[QK_ROPE:]
        else:
            pe = pe[:QK_ROPE]
        qt_ref[base:base + QK_NOPE, :] = (qt[:QK_NOPE] * scale).astype(BF16)
        qt_ref[base + QK_NOPE:base + QK_NOPE + QK_ROPE, :] = (pe * scale).astype(BF16)
        qt_ref[base + QK_NOPE + QK_ROPE:base + HEAD_QK, :] = zeros


def _q_proj(h, w_in, li, qa_block, q_norm_g, w_qbt, cst, *, tm, seq, heads, scale):
    n_tok, d = h.shape
    ql = q_norm_g.shape[1]
    rope = cst is not None
    in_specs = [
        pl.BlockSpec((tm, d), lambda i: (i, 0)),
        pl.BlockSpec((None, d, ql), lambda i: (li, 0, qa_block), pipeline_mode=pl.Buffered(1)),
        pl.BlockSpec((1, ql), lambda i: (0, 0)),
        pl.BlockSpec((None, heads * HEAD_QK, ql), lambda i: (li, 0, 0), pipeline_mode=pl.Buffered(1)),
    ]
    args = [h, w_in, q_norm_g, w_qbt]
    if rope:
        nb = seq // tm
        in_specs.append(pl.BlockSpec((LANES, tm), lambda i: (0, i % nb)))
        args.append(cst)
    return pl.pallas_call(
        functools.partial(_q_kernel, heads=heads, rope=rope, scale=scale),
        out_shape=jax.ShapeDtypeStruct((heads * HEAD_QK, n_tok), BF16),
        grid=(n_tok // tm,),
        in_specs=in_specs,
        out_specs=pl.BlockSpec((heads * HEAD_QK, tm), lambda i: (0, i)),
        compiler_params=_params("parallel"),
        name="mla_q_proj",
    )(*args)


HEAD_GROUP = 4


def _write_kv(ckv, kper, wk_ref, wvt_ref, k_ref, vt_ref, heads):
    cb = ckv.astype(BF16)
    cbt = ckv.T.astype(BF16)
    gw = HEAD_GROUP * V_HEAD
    for g in range(heads // HEAD_GROUP):
        kg = _dot(cb, wk_ref[:, g * gw:(g + 1) * gw])
        for j in range(HEAD_GROUP):
            base = (g * HEAD_GROUP + j) * HEAD_QK
            k_ref[:, base:base + QK_NOPE] = kg[:, j * QK_NOPE:(j + 1) * QK_NOPE].astype(BF16)
            k_ref[:, base + QK_NOPE:base + HEAD_QK] = kper
        vt_ref[g * gw:(g + 1) * gw, :] = _dot(wvt_ref[g * gw:(g + 1) * gw, :], cbt).astype(BF16)


def _kv_kernel(*refs, heads, rope, emit):
    refs = list(refs)
    h_ref, wckv_ref, kg_ref, wkpe_ref, wk_ref, wvt_ref = refs[:6]
    rest = refs[6:]
    cs_ref = rest.pop(0) if rope else None
    k_ref, vt_ref = rest[:2]
    h = h_ref[...]
    ckv = _rms(_dot(h, wckv_ref[...]), kg_ref[...])
    kp = _dot(h, wkpe_ref[...])
    if emit:
        ckv_ref, kpe_ref = rest[2:]
        ckv_ref[...] = ckv
        kpe_ref[...] = kp[:, :QK_ROPE]
    kk = _rope_pair(kp, cs_ref[...]) if rope else kp
    lane = lax.broadcasted_iota(jnp.int32, kk.shape, 1)
    kper = jnp.where(lane < QK_ROPE, kk, 0.0).astype(BF16)
    _write_kv(ckv, kper, wk_ref, wvt_ref, k_ref, vt_ref, heads)


def _kv_proj(h, w_in, li, ckv_block, kv_norm_g, w_kpe2, w_k, w_vt, cs, *, tm, seq, heads, emit):
    n_tok, d = h.shape
    kvl = kv_norm_g.shape[1]
    rope = cs is not None
    in_specs = [
        pl.BlockSpec((tm, d), lambda i: (i, 0)),
        pl.BlockSpec((None, d, kvl), lambda i: (li, 0, ckv_block), pipeline_mode=pl.Buffered(1)),
        pl.BlockSpec((1, kvl), lambda i: (0, 0)),
        pl.BlockSpec((d, LANES), lambda i: (0, 0)),
        pl.BlockSpec((None, kvl, heads * QK_NOPE), lambda i: (li, 0, 0), pipeline_mode=pl.Buffered(1)),
        pl.BlockSpec((None, heads * V_HEAD, kvl), lambda i: (li, 0, 0), pipeline_mode=pl.Buffered(1)),
    ]
    args = [h, w_in, kv_norm_g, w_kpe2, w_k, w_vt]
    if rope:
        nb = seq // tm
        in_specs.append(pl.BlockSpec((tm, LANES), lambda i: (i % nb, 0)))
        args.append(cs)
    out_shape = [jax.ShapeDtypeStruct((n_tok, heads * HEAD_QK), BF16),
                 jax.ShapeDtypeStruct((heads * V_HEAD, n_tok), BF16)]
    out_specs = [pl.BlockSpec((tm, heads * HEAD_QK), lambda i: (i, 0)),
                 pl.BlockSpec((heads * V_HEAD, tm), lambda i: (0, i))]
    if emit:
        out_shape += [jax.ShapeDtypeStruct((n_tok, kvl), F32), jax.ShapeDtypeStruct((n_tok, QK_ROPE), F32)]
        out_specs += [pl.BlockSpec((tm, kvl), lambda i: (i, 0)), pl.BlockSpec((tm, QK_ROPE), lambda i: (i, 0))]
    return pl.pallas_call(
        functools.partial(_kv_kernel, heads=heads, rope=rope, emit=emit),
        out_shape=tuple(out_shape),
        grid=(n_tok // tm,),
        in_specs=in_specs,
        out_specs=tuple(out_specs),
        compiler_params=_params("parallel"),
        name="mla_kv_proj",
    )(*args)


def _cache_kv_kernel(c_ref, kpe_ref, wk_ref, wvt_ref, k_ref, vt_ref, *, heads):
    _write_kv(c_ref[...], kpe_ref[...], wk_ref, wvt_ref, k_ref, vt_ref, heads)


def _cache_kv(ckv, kpe_pad, w_k, w_vt, li, *, tm, heads):
    n, kvl = ckv.shape
    return pl.pallas_call(
        functools.partial(_cache_kv_kernel, heads=heads),
        out_shape=(jax.ShapeDtypeStruct((n, heads * HEAD_QK), BF16),
                   jax.ShapeDtypeStruct((heads * V_HEAD, n), BF16)),
        grid=(n // tm,),
        in_specs=[
            pl.BlockSpec((tm, kvl), lambda i: (i, 0)),
            pl.BlockSpec((tm, LANES), lambda i: (i, 0)),
            pl.BlockSpec((None, kvl, heads * QK_NOPE), lambda i: (li, 0, 0), pipeline_mode=pl.Buffered(1)),
            pl.BlockSpec((None, heads * V_HEAD, kvl), lambda i: (li, 0, 0), pipeline_mode=pl.Buffered(1)),
        ],
        out_specs=(pl.BlockSpec((tm, heads * HEAD_QK), lambda i: (i, 0)),
                   pl.BlockSpec((heads * V_HEAD, tm), lambda i: (0, i))),
        compiler_params=_params("parallel"),
        name="mla_cache_kv",
    )(ckv, kpe_pad, w_k, w_vt)


def _gate_kernel(h_ref, w_ref, o_ref, *, tn):
    h = h_ref[...]
    for c in range(o_ref.shape[1] // tn):
        sl = slice(c * tn, (c + 1) * tn)
        o_ref[:, sl] = _silu(_dot(h, w_ref[:, sl])).astype(BF16)


def _gate(h, w, *, tm, tn):
    n_tok, d = h.shape
    n = w.shape[1]
    return pl.pallas_call(
        functools.partial(_gate_kernel, tn=tn),
        out_shape=jax.ShapeDtypeStruct((n_tok, n), BF16),
        grid=(n_tok // tm,),
        in_specs=[pl.BlockSpec((tm, d), lambda i: (i, 0)), pl.BlockSpec((d, n), lambda i: (0, 0))],
        out_specs=pl.BlockSpec((tm, n), lambda i: (i, 0)),
        compiler_params=_params("parallel"),
        name="mla_gate",
    )(h, w)


def _fold8(x, op):
    return op(x.reshape(x.shape[0] // SUBLANES, SUBLANES, x.shape[1]), axis=0)


def _attn_kernel(qt_ref, k_ref, vt_ref, gate_ref, o_ref, *, heads):
    for hd in range(heads):
        qk = slice(hd * HEAD_QK, (hd + 1) * HEAD_QK)
        cols = slice(hd * V_HEAD, (hd + 1) * V_HEAD)
        st = _dot(k_ref[:, qk], qt_ref[qk, :])
        pt = jnp.exp2(st - jnp.max(st, axis=0, keepdims=True))
        ot = _dot(vt_ref[cols, :], pt.astype(BF16)) / jnp.sum(pt, axis=0, keepdims=True)
        o_ref[:, cols] = (ot.T * gate_ref[:, cols].astype(F32)).astype(BF16)


def _attention(qt, k, vt, gate, *, batch, seq, heads):
    n_tok = k.shape[0]
    return pl.pallas_call(
        functools.partial(_attn_kernel, heads=heads),
        out_shape=jax.ShapeDtypeStruct((n_tok, heads * V_HEAD), BF16),
        grid=(batch,),
        in_specs=[
            pl.BlockSpec((heads * HEAD_QK, seq), lambda b: (0, b)),
            pl.BlockSpec((seq, heads * HEAD_QK), lambda b: (b, 0)),
            pl.BlockSpec((heads * V_HEAD, seq), lambda b: (0, b)),
            pl.BlockSpec((seq, heads * V_HEAD), lambda b: (b, 0)),
        ],
        out_specs=pl.BlockSpec((seq, heads * V_HEAD), lambda b: (b, 0)),
        compiler_params=_params("parallel"),
        name="mla_attention",
    )(qt, k, vt, gate)


def _attn_pipe_kernel(qt_ref, kc_ref, ko_ref, vtc_ref, vto_ref, gate_ref, o_ref, s_scr, m_scr, *, kc, nsub):
    @pl.when(pl.program_id(0) == 0)
    def _():
        s_scr[...] = jnp.zeros_like(s_scr)
        m_scr[...] = jnp.zeros_like(m_scr)

    tq = qt_ref.shape[1] // nsub
    half = tq // 2
    for u in range(nsub):
        qcols = slice(u * tq, (u + 1) * tq)
        qa = qt_ref[:, u * tq:u * tq + half]
        qb = qt_ref[:, u * tq + half:(u + 1) * tq]
        m_old = m_scr[u][0:1]
        acc = l_acc = m_acc = None
        off = 0
        for k_ref, vt_ref in ((kc_ref, vtc_ref), (ko_ref, vto_ref)):
            for c in range(k_ref.shape[0] // kc):
                rows = slice(c * kc, (c + 1) * kc)
                srows = slice(off, off + kc)
                pt = jnp.exp2(s_scr[u, srows, :] - m_old)
                lsum = _fold8(pt, jnp.sum)
                l_acc = lsum if l_acc is None else l_acc + lsum
                pv = _dot(vt_ref[:, rows], pt.astype(BF16))
                acc = pv if acc is None else acc + pv
                k_c = k_ref[rows, :]
                sa = _dot(k_c, qa)
                sb = _dot(k_c, qb)
                s_scr[u, srows, 0:half] = sa
                s_scr[u, srows, half:tq] = sb
                mx = jnp.concatenate([_fold8(sa, jnp.max), _fold8(sb, jnp.max)], axis=1)
                m_acc = mx if m_acc is None else jnp.maximum(m_acc, mx)
                off += kc
        m_scr[u] = jnp.broadcast_to(jnp.max(m_acc, axis=0, keepdims=True), (SUBLANES, tq))
        ot = acc / jnp.sum(l_acc, axis=0, keepdims=True)
        o_ref[qcols, :] = (ot.T * gate_ref[qcols, :].astype(F32)).astype(BF16)


def _attention_pipelined(qt, kc, vtc, ko, vto, gate, *, batch, seq, past, heads, tq, nsub, kc_size):
    n_tok = ko.shape[0]
    tsub, tq = tq, tq * nsub
    nq = seq // tq
    per_b = heads * nq
    jobs = batch * per_b
    jq = lambda g: jnp.minimum(g, jobs - 1)
    jp = lambda g: jnp.maximum(g - 1, 0)
    b_of = lambda j: j // per_b
    hd_of = lambda j: (j % per_b) // nq
    row_of = lambda j: b_of(j) * nq + j % nq
    return pl.pallas_call(
        functools.partial(_attn_pipe_kernel, kc=kc_size, nsub=nsub),
        out_shape=jax.ShapeDtypeStruct((n_tok, heads * V_HEAD), BF16),
        grid=(jobs + 1,),
        in_specs=[
            pl.BlockSpec((HEAD_QK, tq), lambda g: (hd_of(jq(g)), row_of(jq(g)))),
            pl.BlockSpec((past, HEAD_QK), lambda g: (b_of(jq(g)), hd_of(jq(g)))),
            pl.BlockSpec((seq, HEAD_QK), lambda g: (b_of(jq(g)), hd_of(jq(g)))),
            pl.BlockSpec((V_HEAD, past), lambda g: (hd_of(jp(g)), b_of(jp(g)))),
            pl.BlockSpec((V_HEAD, seq), lambda g: (hd_of(jp(g)), b_of(jp(g)))),
            pl.BlockSpec((tq, V_HEAD), lambda g: (row_of(jp(g)), hd_of(jp(g)))),
        ],
        out_specs=pl.BlockSpec((tq, V_HEAD), lambda g: (row_of(jp(g)), hd_of(jp(g)))),
        scratch_shapes=[pltpu.VMEM((nsub, past + seq, tsub), F32), pltpu.VMEM((nsub, SUBLANES, tsub), F32)],
        compiler_params=_params("arbitrary"),
        name="mla_attention_pipelined",
    )(qt, kc, ko, vtc, vto, gate)


def _out_kernel(*refs, tn, last):
    if last:
        y1_ref, y2_ref, w1_ref, w2_ref, x_ref, g_ref, ng_ref, o_ref = refs
    else:
        y1_ref, y2_ref, w1_ref, w2_ref, x_ref, g_ref, ng_ref, sh_ref, sc_ref, o_ref, h_ref = refs
    d = x_ref.shape[1]
    y1 = y1_ref[...]
    y2 = y2_ref[...]
    g = g_ref[0]
    ss = None
    for c in range(d // tn):
        sl = slice(c * tn, (c + 1) * tn)
        acc = _dot(y1, w1_ref[:, sl]) + _dot(y2, w2_ref[:, sl])
        xn = x_ref[:, sl] + g[:, sl] * acc
        o_ref[:, sl] = xn
        part = jnp.sum(xn * xn, axis=-1, keepdims=True)
        ss = part if ss is None else ss + part
    inv = lax.rsqrt(ss / d + EPS)
    for c in range(d // tn):
        sl = slice(c * tn, (c + 1) * tn)
        normed = o_ref[:, sl] * inv * ng_ref[:, sl]
        if last:
            o_ref[:, sl] = normed
        else:
            h_ref[:, sl] = (normed * (1.0 + sc_ref[0][:, sl]) + sh_ref[0][:, sl]).astype(BF16)


def _out_proj(y1, y2, c1, c2, w_out, li, x, mod_l, next_g, mod_next, *, tm, tn, row_of):
    n_tok, d = x.shape
    kh = w_out.shape[1] // 2
    last = mod_next is None
    in_specs = [
        pl.BlockSpec((tm, kh), lambda i: (i, c1)),
        pl.BlockSpec((tm, kh), lambda i: (i, c2)),
        pl.BlockSpec((None, kh, d), lambda i: (li, 0, 0), pipeline_mode=pl.Buffered(1)),
        pl.BlockSpec((None, kh, d), lambda i: (li, 1, 0), pipeline_mode=pl.Buffered(1)),
        pl.BlockSpec((tm, d), lambda i: (i, 0)),
        pl.BlockSpec((1, 1, d), lambda i: (row_of(i * tm), 0, 2)),
        pl.BlockSpec((1, d), lambda i: (0, 0)),
    ]
    args = [y1, y2, w_out, w_out, x, mod_l, next_g]
    out_shape = [jax.ShapeDtypeStruct((n_tok, d), F32)]
    out_specs = [pl.BlockSpec((tm, d), lambda i: (i, 0))]
    if not last:
        in_specs += [pl.BlockSpec((1, 1, d), lambda i: (row_of(i * tm), 0, 0)),
                     pl.BlockSpec((1, 1, d), lambda i: (row_of(i * tm), 0, 1))]
        args += [mod_next, mod_next]
        out_shape.append(jax.ShapeDtypeStruct((n_tok, d), BF16))
        out_specs.append(pl.BlockSpec((tm, d), lambda i: (i, 0)))
    res = pl.pallas_call(
        functools.partial(_out_kernel, tn=tn, last=last),
        out_shape=tuple(out_shape),
        grid=(n_tok // tm,),
        in_specs=in_specs,
        out_specs=tuple(out_specs),
        compiler_params=_params("parallel"),
        name="out_proj",
    )(*args)
    return res[0] if last else res


def _v_ln_kernel(h_ref, wv_ref, lg_ref, lb_ref, vn_ref, vacc, mu_scr, *, nj, tnv, n):
    @pl.when(pl.program_id(0) == 0)
    def _():
        vacc[...] = jnp.zeros_like(vacc)
        mu_scr[...] = jnp.zeros_like(mu_scr)

    width = nj * tnv

    def prev_stats():
        mu = mu_scr[:, 0:1]
        sq = functools.reduce(
            jnp.add, [jnp.sum((vacc[t] - mu) * (vacc[t] - mu), axis=-1, keepdims=True) for t in range(nj)])
        return mu, lax.rsqrt(sq / width + EPS)

    def norm_prev(t, mu, inv):
        sl = slice(t * tnv, (t + 1) * tnv)
        vn_ref[:, sl] = ((vacc[t] - mu) * inv * lg_ref[:, sl] + lb_ref[:, sl]).astype(BF16)

    @pl.when(pl.program_id(0) < n)
    def _():
        mu, inv = prev_stats()
        h = h_ref[...]
        tot = None
        for t in range(nj):
            norm_prev(t, mu, inv)
            v_new = _dot(h, wv_ref[:, t * tnv:(t + 1) * tnv])
            vacc[t] = v_new
            part = jnp.sum(v_new, axis=-1, keepdims=True)
            tot = part if tot is None else tot + part
        mu_scr[...] = jnp.broadcast_to(tot / width, mu_scr.shape)

    @pl.when(pl.program_id(0) == n)
    def _():
        mu, inv = prev_stats()
        for t in range(nj):
            norm_prev(t, mu, inv)


def _v_ln(h, w_in, li, ln_g, ln_b, *, tm, tnv):
    n_tok, d = h.shape
    sgw = ln_g.shape[1]
    nj = sgw // tnv
    n = n_tok // tm
    return pl.pallas_call(
        functools.partial(_v_ln_kernel, nj=nj, tnv=tnv, n=n),
        out_shape=jax.ShapeDtypeStruct((n_tok, sgw), BF16),
        grid=(n + 1,),
        in_specs=[
            pl.BlockSpec((tm, d), lambda i: (jnp.minimum(i, n - 1), 0)),
            pl.BlockSpec((None, d, sgw), lambda i: (li, 0, 1), pipeline_mode=pl.Buffered(1)),
            pl.BlockSpec((1, sgw), lambda i: (0, 0)),
            pl.BlockSpec((1, sgw), lambda i: (0, 0)),
        ],
        out_specs=pl.BlockSpec((tm, sgw), lambda i: (jnp.maximum(i - 1, 0), 0)),
        scratch_shapes=[pltpu.VMEM((nj, tm, tnv), F32), pltpu.VMEM((tm, LANES), F32)],
        compiler_params=_params("arbitrary"),
        name="odd_v_ln",
    )(h, w_in, ln_g, ln_b)


def _sg_kernel(h_ref, wu_ref, wg_ref, vn_ref, ws_ref, bs_ref, y_ref, *, tm, chunk, gps):
    h = h_ref[...]
    gw = y_ref.shape[1] // gps
    for s in range(gps):
        cols = slice(s * gw, (s + 1) * gw)
        u = _dot(h, wu_ref[:, cols])
        g = _dot(h, wg_ref[:, cols])
        ws = ws_ref[s]
        bs = bs_ref[s]
        for c in range(tm // chunk):
            sl = slice(c * chunk, (c + 1) * chunk)
            vs = _dot(ws, vn_ref[sl, cols]) + bs
            y_ref[sl, cols] = (u[sl] * vs * _silu(g[sl])).astype(BF16)


def _spatial_gate(h, w_in, li, vn, w_s, b_s, *, tm, gps):
    n_tok, d = h.shape
    _, groups, chunk, _ = w_s.shape
    sgw = vn.shape[1]
    tn = gps * sgw // groups
    nj = groups // gps
    return pl.pallas_call(
        functools.partial(_sg_kernel, tm=tm, chunk=chunk, gps=gps),
        out_shape=jax.ShapeDtypeStruct((n_tok, sgw), BF16),
        grid=(n_tok // tm, nj),
        in_specs=[
            pl.BlockSpec((tm, d), lambda i, j: (i, 0)),
            pl.BlockSpec((None, d, tn), lambda i, j: (li, 0, j)),
            pl.BlockSpec((None, d, tn), lambda i, j: (li, 0, 2 * nj + j)),
            pl.BlockSpec((tm, tn), lambda i, j: (i, j)),
            pl.BlockSpec((None, gps, chunk, chunk), lambda i, j: (li, j, 0, 0)),
            pl.BlockSpec((None, gps, chunk, 1), lambda i, j: (li, j, 0, 0)),
        ],
        out_specs=pl.BlockSpec((tm, tn), lambda i, j: (i, j)),
        compiler_params=_params("parallel", "arbitrary"),
        name="odd_spatial_gate",
    )(h, w_in, w_in, vn, w_s, b_s)


def _rot_cols(w):
    shp = w.shape
    w4 = w.reshape(shp[:-1] + (2, 2, QK_ROPE // 4))
    return jnp.stack([-w4[..., 1, :], w4[..., 0, :]], axis=-2).reshape(shp)


def _rope_table(n):
    f32 = np.float32
    rows = n // GRID_W
    row = np.repeat(np.arange(rows, dtype=f32), GRID_W)
    col = np.tile(np.arange(GRID_W, dtype=f32), rows)
    n_freq = QK_ROPE // 4
    inv = np.power(f32(ROPE_BASE), -np.arange(n_freq, dtype=f32) / f32(n_freq)).astype(f32)
    ar = row[:, None] * inv
    ac = col[:, None] * inv
    ang = np.concatenate([ar, ar, ac, ac], axis=-1).astype(f32)
    cs = np.concatenate([np.cos(ang), np.sin(ang)], axis=-1).astype(f32)
    return jnp.asarray(cs), jnp.asarray(np.ascontiguousarray(cs.T))


def _tile(n, pref):
    return pref if n % pref == 0 else n


def _prep_weights(e_w_in, e_w_qb, e_w_kvb, e_w_out, o_w_in, o_w_s, o_w_out, cw, ql, kvl, heads):
    n_even = e_w_in.shape[0]
    o = 4 * cw + ql + kvl
    e_in = e_w_in.astype(BF16)
    w_kpe = e_in[:, :, o:o + QK_ROPE]
    n = np.arange(heads * HEAD_QK)
    hd, r = n // HEAD_QK, n % HEAD_QK
    j = r - QK_NOPE - QK_ROPE
    first = (j % (QK_ROPE // 2)) < QK_ROPE // 4
    src = np.where(j < 0, r, QK_NOPE + np.where(first, j + QK_ROPE // 4, j - QK_ROPE // 4))
    src = hd * (QK_NOPE + QK_ROPE) + src
    sign = np.where((j >= 0) & first, -1.0, 1.0).astype(np.float32)
    cols = lax.broadcasted_iota(jnp.int32, (heads * HEAD_QK, heads * (QK_NOPE + QK_ROPE)), 1)
    sel = jnp.where(cols == jnp.asarray(src, jnp.int32)[:, None], jnp.asarray(sign)[:, None], 0.0).astype(BF16)
    w_qbt = jnp.einsum("nc,lkc->lnk", sel, e_w_qb.astype(BF16), preferred_element_type=F32).astype(BF16)
    w_kvb = e_w_kvb.astype(BF16).reshape(n_even, kvl, heads, 2, V_HEAD)
    return dict(
        e_in=e_in,
        w_kpe2=jnp.concatenate([w_kpe, _rot_cols(w_kpe)], axis=-1),
        w_mg=e_in[:, :, o + QK_ROPE:],
        w_qbt=w_qbt,
        w_k=w_kvb[:, :, :, 0, :].reshape(n_even, kvl, heads * QK_NOPE),
        w_vt=w_kvb[:, :, :, 1, :].reshape(n_even, kvl, heads * V_HEAD).transpose(0, 2, 1),
        e_out=e_w_out.astype(BF16),
        o_in=o_w_in.astype(BF16),
        w_s=o_w_s.astype(BF16),
        o_out=o_w_out.astype(BF16),
    )


def _trunk(x, *, batch, seq, cond_row, cs, cache, mod, norm_g, w, e_conv_w, e_q_norm_g, e_kv_norm_g,
           o_ln_g, o_ln_b, o_b_s, final_g, heads, emit):
    n_tok, d = x.shape
    depth = mod.shape[0]
    cw = e_conv_w.shape[2]
    ql = e_q_norm_g.shape[1]
    kvl = e_kv_norm_g.shape[1]
    assert seq & (seq - 1) == 0, "sequence length must be a power of two"
    cs, cst = (None, None) if cs is None else cs
    assert heads % HEAD_GROUP == 0 and (4 * cw) % ql == 0 and (4 * cw + ql) % kvl == 0 and cw == heads * V_HEAD
    tm = _tile(n_tok, 512)
    tml = _tile(n_tok, 1024)
    scale = float((QK_NOPE + QK_ROPE) ** -0.5 * math.log2(math.e))
    row_of = cond_row
    mods = [mod[l].reshape(mod.shape[1], 1, mod.shape[2]) for l in range(depth)]
    b_s = o_b_s[..., None]
    ckvs, kpes = [], []
    h = _first_h(x, mods[0], norm_g[0][None, :], tm=tm, row_of=row_of)
    for l in range(depth):
        i = l // 2
        if l % 2 == 0:
            y1 = _conv_branch(h, w["e_in"], i, e_conv_w[i], tm=tml, tn=_tile(cw, 512), seq=seq,
                              nsub=2 if _tile(cw, 512) % 512 == 0 else 1)
            q = _q_proj(h, w["e_in"], i, (4 * cw) // ql, e_q_norm_g[i][None, :], w["w_qbt"],
                        cst, tm=tml, seq=seq, heads=heads, scale=scale)
            kv = _kv_proj(h, w["e_in"], i, (4 * cw + ql) // kvl, e_kv_norm_g[i][None, :], w["w_kpe2"][i],
                          w["w_k"], w["w_vt"], cs, tm=tml, seq=seq, heads=heads, emit=emit)
            if emit:
                ckvs.append(kv[2])
                kpes.append(kv[3])
            gate = _gate(h, w["w_mg"][i], tm=tml, tn=_tile(w["w_mg"].shape[2], 512))
            if cache is None:
                y2 = _attention(q, kv[0], kv[1], gate, batch=batch, seq=seq, heads=heads)
            else:
                c_ckv, c_kpe = cache
                past = c_ckv.shape[2]
                kpe_pad = jnp.pad(c_kpe[:, i].reshape(batch * past, QK_ROPE),
                                  ((0, 0), (0, LANES - QK_ROPE))).astype(BF16)
                kc, vc = _cache_kv(c_ckv[:, i].reshape(batch * past, kvl), kpe_pad, w["w_k"], w["w_vt"], i,
                                   tm=_tile(batch * past, 512), heads=heads)
                y2 = _attention_pipelined(q, kc, vc, kv[0], kv[1], gate, batch=batch, seq=seq, past=past,
                                          heads=heads, tq=_tile(seq // 4, 512), nsub=4,
                                          kc_size=_tile(math.gcd(past, seq), 256))
            c2, w_out = 0, w["e_out"]
        else:
            vn = _v_ln(h, w["o_in"], i, o_ln_g[i][None, :], o_ln_b[i][None, :], tm=tm,
                       tnv=_tile(o_ln_g.shape[1], 256))
            groups = w["w_s"].shape[1]
            y1 = y2 = _spatial_gate(h, w["o_in"], i, vn, w["w_s"], b_s, tm=tml, gps=4 if groups % 4 == 0 else 1)
            c2, w_out = 1, w["o_out"]
        if l + 1 < depth:
            x, h = _out_proj(y1, y2, 0, c2, w_out, i, x, mods[l], norm_g[l + 1][None, :], mods[l + 1],
                             tm=tm, tn=_tile(d, 512), row_of=row_of)
        else:
            x = _out_proj(y1, y2, 0, c2, w_out, i, x, mods[l], final_g[None, :], None,
                          tm=tm, tn=_tile(d, 512), row_of=row_of)
    return x, ckvs, kpes


def kernel(x_prompt, x_sample, cache_ckv, cache_kpe, c, c_ctx, norm_g, w_ada, b_ada, e_w_in, e_conv_w,
           e_q_norm_g, e_w_qb, e_kv_norm_g, e_w_kvb, e_w_out, o_w_in, o_ln_g, o_ln_b, o_w_s, o_b_s, o_w_out,
           final_g):
    batch, seq, d = x_prompt.shape
    dec_batch, dec_seq, _ = x_sample.shape
    cw = e_conv_w.shape[2]
    ql = e_q_norm_g.shape[1]
    kvl = e_kv_norm_g.shape[1]
    heads = e_w_qb.shape[2] // (QK_NOPE + QK_ROPE)

    rows = -(-(dec_batch + 1) // SUBLANES) * SUBLANES
    cond = jnp.concatenate([c, c_ctx[None, :], jnp.zeros((rows - dec_batch - 1, d), F32)], axis=0)
    mod = _ada(cond, w_ada, b_ada)

    w = _prep_weights(e_w_in, e_w_qb, e_w_kvb, e_w_out, o_w_in, o_w_s, o_w_out, cw, ql, kvl, heads)
    shared = dict(mod=mod, norm_g=norm_g, w=w, e_conv_w=e_conv_w, e_q_norm_g=e_q_norm_g,
                  e_kv_norm_g=e_kv_norm_g, o_ln_g=o_ln_g, o_ln_b=o_ln_b, o_b_s=o_b_s,
                  final_g=final_g, heads=heads)

    y_prompt, ckvs, kpes = _trunk(x_prompt.reshape(batch * seq, d), batch=batch, seq=seq,
                                  cond_row=lambda r: dec_batch, cs=None, cache=None, emit=True, **shared)
    y_sample, _, _ = _trunk(x_sample.reshape(dec_batch * dec_seq, d), batch=dec_batch, seq=dec_seq,
                            cond_row=lambda r: r // dec_seq, cs=_rope_table(dec_seq),
                            cache=(cache_ckv, cache_kpe), emit=False, **shared)
    new_ckv = jnp.stack([t.reshape(batch, seq, kvl) for t in ckvs], axis=1)
    new_kpe = jnp.stack([t.reshape(batch, seq, QK_ROPE) for t in kpes], axis=1)
    return (y_prompt.reshape(batch, seq, d), y_sample.reshape(dec_batch, dec_seq, d), new_ckv, new_kpe)
```

```python
import functools
import math

import jax
import jax.numpy as jnp
import numpy as np
from jax import lax
from jax.experimental import pallas as pl
from jax.experimental.pallas import tpu as pltpu

F32 = jnp.float32
BF16 = jnp.bfloat16

EPS = 1e-6
QK_NOPE = 128
QK_ROPE = 64
V_HEAD = 128
HEAD_QK = 256
GRID_W = 64
ROPE_BASE = 10000.0

V7X_VMEM_BYTES = 64 * 1024 * 1024
VMEM_LIMIT = V7X_VMEM_BYTES - 8 * 1024 * 1024
SUBLANES = 8
PACKED_ROWS = 16
LANES = 128


def _silu(x):
    return x * jax.nn.sigmoid(x)


def _dot(a, b):
    return jnp.dot(a, b, preferred_element_type=F32)


def _rms(x, g):
    return x * lax.rsqrt(jnp.mean(x * x, axis=-1, keepdims=True) + EPS) * g


def _params(*sem):
    return pltpu.CompilerParams(dimension_semantics=sem, vmem_limit_bytes=VMEM_LIMIT)


def _ada_kernel(c_ref, w_ref, b_ref, o_ref):
    a = _silu(c_ref[...]).astype(BF16)
    o_ref[0] = _dot(a, w_ref[0].astype(BF16)) + b_ref[0]


def _ada(cond, w_ada, b_ada):
    depth, d, n = w_ada.shape
    rows = cond.shape[0]
    tn = 768 if n % 768 == 0 else n
    return pl.pallas_call(
        _ada_kernel,
        out_shape=jax.ShapeDtypeStruct((depth, rows, n), F32),
        grid=(depth, n // tn),
        in_specs=[
            pl.BlockSpec((rows, d), lambda l, j: (0, 0)),
            pl.BlockSpec((1, d, tn), lambda l, j: (l, 0, j)),
            pl.BlockSpec((1, 1, tn), lambda l, j: (l, 0, j)),
        ],
        out_specs=pl.BlockSpec((1, rows, tn), lambda l, j: (l, 0, j)),
        compiler_params=_params("parallel", "parallel"),
        name="ada_mod",
    )(cond, w_ada, b_ada.reshape(depth, 1, n))


def _norm_mod(x, g, sc, sh):
    return _rms(x, g) * (1.0 + sc) + sh


def _norm_mod_kernel(x_ref, ng_ref, sh_ref, sc_ref, h_ref):
    h_ref[...] = _norm_mod(x_ref[...], ng_ref[...], sc_ref[0], sh_ref[0]).astype(BF16)


def _first_h(x, mod_l, norm_g, *, tm, row_of):
    n_tok, d = x.shape
    return pl.pallas_call(
        _norm_mod_kernel,
        out_shape=jax.ShapeDtypeStruct((n_tok, d), BF16),
        grid=(n_tok // tm,),
        in_specs=[
            pl.BlockSpec((tm, d), lambda i: (i, 0)),
            pl.BlockSpec((1, d), lambda i: (0, 0)),
            pl.BlockSpec((1, 1, d), lambda i: (row_of(i * tm), 0, 0)),
            pl.BlockSpec((1, 1, d), lambda i: (row_of(i * tm), 0, 1)),
        ],
        out_specs=pl.BlockSpec((tm, d), lambda i: (i, 0)),
        compiler_params=_params("parallel"),
        name="norm_mod",
    )(x, norm_g, mod_l, mod_l)


def _conv_kernel(h_ref, hp_ref, hn_ref, wb_ref, wc_ref, wx_ref, wg_ref, cw_ref, y_ref, hx_scr, *, tm, seq, nsub):
    i = pl.program_id(0)
    j = pl.program_id(1)

    @pl.when(j == 0)
    def _():
        hx_scr[0:tm] = h_ref[...]
        hx_scr[tm:tm + PACKED_ROWS] = hp_ref[...]
        hx_scr[tm + PACKED_ROWS:tm + 2 * PACKED_ROWS] = hn_ref[...]

    he = hx_scr[...]
    hm = h_ref[...]
    ts = y_ref.shape[1] // nsub
    for s in range(nsub):
        cols = slice(s * ts, (s + 1) * ts)
        pe = _dot(he, wc_ref[:, cols]) * _dot(he, wx_ref[:, cols])
        p = pe[:tm]
        p_before = pe[tm + PACKED_ROWS - 1:tm + PACKED_ROWS]
        p_after = pe[tm + PACKED_ROWS:tm + PACKED_ROWS + 1]
        cb = _dot(hm, wb_ref[:, cols])
        cg = _dot(hm, wg_ref[:, cols])

        row = lax.broadcasted_iota(jnp.int32, p.shape, 0)
        pos = (i * tm + row) & (seq - 1)
        prev = jnp.where(row == 0, p_before, pltpu.roll(p, 1, 0))
        prev = jnp.where(pos == 0, 0.0, prev)
        nxt = jnp.where(row == tm - 1, p_after, pltpu.roll(p, tm - 1, 0))
        nxt = jnp.where(pos == seq - 1, 0.0, nxt)
        cw = cw_ref[:, cols]
        conv = prev * cw[0:1] + p * cw[1:2] + nxt * cw[2:3]
        y_ref[:, cols] = (cb * conv * _silu(cg)).astype(BF16)


def _conv_branch(h, w_in, li, conv_w, *, tm, tn, seq, nsub):
    n_tok, d = h.shape
    cw = conv_w.shape[1]
    nj = cw // tn
    n16 = n_tok // PACKED_ROWS
    t16 = tm // PACKED_ROWS
    wspec = lambda grp: pl.BlockSpec((None, d, tn), lambda i, j: (li, 0, grp * nj + j))
    return pl.pallas_call(
        functools.partial(_conv_kernel, tm=tm, seq=seq, nsub=nsub),
        out_shape=jax.ShapeDtypeStruct((n_tok, cw), BF16),
        grid=(n_tok // tm, nj),
        in_specs=[
            pl.BlockSpec((tm, d), lambda i, j: (i, 0)),
            pl.BlockSpec((PACKED_ROWS, d), lambda i, j: (jnp.maximum(i * t16 - 1, 0), 0)),
            pl.BlockSpec((PACKED_ROWS, d), lambda i, j: (jnp.minimum((i + 1) * t16, n16 - 1), 0)),
            wspec(0), wspec(1), wspec(2), wspec(3),
            pl.BlockSpec((3, tn), lambda i, j: (0, j)),
        ],
        out_specs=pl.BlockSpec((tm, tn), lambda i, j: (i, j)),
        scratch_shapes=[pltpu.VMEM((tm + 2 * PACKED_ROWS, d), BF16)],
        compiler_params=_params("parallel", "arbitrary"),
        name="even_conv",
    )(h, h, h, w_in, w_in, w_in, w_in, conv_w)


def _rope_pair(v, cs):
    t = v * cs
    return t + pltpu.roll(t, QK_ROPE, 1)


def _q_kernel(*refs, heads, rope, scale):
    if rope:
        h_ref, wqa_ref, qg_ref, wqbt_ref, wmg_ref, cst_ref, qt_ref, gate_ref = refs
    else:
        h_ref, wqa_ref, qg_ref, wqbt_ref, wmg_ref, qt_ref, gate_ref = refs
    h = h_ref[...]
    gchunk = min(gate_ref.shape[1], 4 * LANES)
    for c in range(gate_ref.shape[1] // gchunk):
        sl = slice(c * gchunk, (c + 1) * gchunk)
        gate_ref[:, sl] = _silu(_dot(h, wmg_ref[:, sl])).astype(BF16)
    qa = _dot(h, wqa_ref[...])
    qnt = _rms(qa, qg_ref[...]).T.astype(BF16)
    zeros = jnp.zeros((HEAD_QK - QK_NOPE - QK_ROPE, qnt.shape[1]), BF16)
    for hd in range(heads):
        base = hd * HEAD_QK
        qt = _dot(wqbt_ref[base:base + HEAD_QK, :], qnt)
        pe = qt[QK_NOPE:]
        if rope:
            t = pe * cst_ref[...]
            pe = t[:QK_ROPE] + t[QK_ROPE:]
        else:
            pe = pe[:QK_ROPE]
        qt_ref[base:base + QK_NOPE, :] = (qt[:QK_NOPE] * scale).astype(BF16)
        qt_ref[base + QK_NOPE:base + QK_NOPE + QK_ROPE, :] = (pe * scale).astype(BF16)
        qt_ref[base + QK_NOPE + QK_ROPE:base + HEAD_QK, :] = zeros


def _q_proj(h, w_in, li, qa_block, q_norm_g, w_qbt, w_mg, cst, *, tm, seq, heads, scale):
    n_tok, d = h.shape
    ql = q_norm_g.shape[1]
    ng = w_mg.shape[1]
    rope = cst is not None
    in_specs = [
        pl.BlockSpec((tm, d), lambda i: (i, 0)),
        pl.BlockSpec((None, d, ql), lambda i: (li, 0, qa_block), pipeline_mode=pl.Buffered(1)),
        pl.BlockSpec((1, ql), lambda i: (0, 0)),
        pl.BlockSpec((None, heads * HEAD_QK, ql), lambda i: (li, 0, 0), pipeline_mode=pl.Buffered(1)),
        pl.BlockSpec((d, ng), lambda i: (0, 0)),
    ]
    args = [h, w_in, q_norm_g, w_qbt, w_mg]
    if rope:
        nb = seq // tm
        in_specs.append(pl.BlockSpec((LANES, tm), lambda i: (0, i % nb)))
        args.append(cst)
    return pl.pallas_call(
        functools.partial(_q_kernel, heads=heads, rope=rope, scale=scale),
        out_shape=(jax.ShapeDtypeStruct((heads * HEAD_QK, n_tok), BF16), jax.ShapeDtypeStruct((n_tok, ng), BF16)),
        grid=(n_tok // tm,),
        in_specs=in_specs,
        out_specs=(pl.BlockSpec((heads * HEAD_QK, tm), lambda i: (0, i)), pl.BlockSpec((tm, ng), lambda i: (i, 0))),
        compiler_params=_params("parallel"),
        name="mla_q_gate_proj",
    )(*args)


HEAD_GROUP = 4


def _write_kv(ckv, kper, wk_ref, wvt_ref, k_ref, vt_ref, heads):
    cb = ckv.astype(BF16)
    cbt = ckv.T.astype(BF16)
    gw = HEAD_GROUP * V_HEAD
    for g in range(heads // HEAD_GROUP):
        kg = _dot(cb, wk_ref[:, g * gw:(g + 1) * gw])
        for j in range(HEAD_GROUP):
            base = (g * HEAD_GROUP + j) * HEAD_QK
            k_ref[:, base:base + QK_NOPE] = kg[:, j * QK_NOPE:(j + 1) * QK_NOPE].astype(BF16)
            k_ref[:, base + QK_NOPE:base + HEAD_QK] = kper
        vt_ref[g * gw:(g + 1) * gw, :] = _dot(wvt_ref[g * gw:(g + 1) * gw, :], cbt).astype(BF16)


def _kv_kernel(*refs, heads, rope, emit):
    refs = list(refs)
    h_ref, wckv_ref, kg_ref, wkpe_ref, wk_ref, wvt_ref = refs[:6]
    rest = refs[6:]
    cs_ref = rest.pop(0) if rope else None
    k_ref, vt_ref = rest[:2]
    h = h_ref[...]
    ckv = _rms(_dot(h, wckv_ref[...]), kg_ref[...])
    kp = _dot(h, wkpe_ref[...])
    if emit:
        ckv_ref, kpe_ref = rest[2:]
        ckv_ref[...] = ckv
        kpe_ref[...] = kp[:, :QK_ROPE]
    kk = _rope_pair(kp, cs_ref[...]) if rope else kp
    lane = lax.broadcasted_iota(jnp.int32, kk.shape, 1)
    kper = jnp.where(lane < QK_ROPE, kk, 0.0).astype(BF16)
    _write_kv(ckv, kper, wk_ref, wvt_ref, k_ref, vt_ref, heads)


def _kv_proj(h, w_in, li, ckv_block, kv_norm_g, w_kpe2, w_k, w_vt, cs, *, tm, seq, heads, emit):
    n_tok, d = h.shape
    kvl = kv_norm_g.shape[1]
    rope = cs is not None
    in_specs = [
        pl.BlockSpec((tm, d), lambda i: (i, 0)),
        pl.BlockSpec((None, d, kvl), lambda i: (li, 0, ckv_block), pipeline_mode=pl.Buffered(1)),
        pl.BlockSpec((1, kvl), lambda i: (0, 0)),
        pl.BlockSpec((d, LANES), lambda i: (0, 0)),
        pl.BlockSpec((None, kvl, heads * QK_NOPE), lambda i: (li, 0, 0), pipeline_mode=pl.Buffered(1)),
        pl.BlockSpec((None, heads * V_HEAD, kvl), lambda i: (li, 0, 0), pipeline_mode=pl.Buffered(1)),
    ]
    args = [h, w_in, kv_norm_g, w_kpe2, w_k, w_vt]
    if rope:
        nb = seq // tm
        in_specs.append(pl.BlockSpec((tm, LANES), lambda i: (i % nb, 0)))
        args.append(cs)
    out_shape = [jax.ShapeDtypeStruct((n_tok, heads * HEAD_QK), BF16),
                 jax.ShapeDtypeStruct((heads * V_HEAD, n_tok), BF16)]
    out_specs = [pl.BlockSpec((tm, heads * HEAD_QK), lambda i: (i, 0)),
                 pl.BlockSpec((heads * V_HEAD, tm), lambda i: (0, i))]
    if emit:
        out_shape += [jax.ShapeDtypeStruct((n_tok, kvl), F32), jax.ShapeDtypeStruct((n_tok, QK_ROPE), F32)]
        out_specs += [pl.BlockSpec((tm, kvl), lambda i: (i, 0)), pl.BlockSpec((tm, QK_ROPE), lambda i: (i, 0))]
    return pl.pallas_call(
        functools.partial(_kv_kernel, heads=heads, rope=rope, emit=emit),
        out_shape=tuple(out_shape),
        grid=(n_tok // tm,),
        in_specs=in_specs,
        out_specs=tuple(out_specs),
        compiler_params=_params("parallel"),
        name="mla_kv_proj",
    )(*args)


def _cache_kv_kernel(c_ref, kpe_ref, wk_ref, wvt_ref, k_ref, vt_ref, *, heads):
    _write_kv(c_ref[...], kpe_ref[...], wk_ref, wvt_ref, k_ref, vt_ref, heads)


def _cache_kv(ckv, kpe_pad, w_k, w_vt, li, *, tm, heads):
    n, kvl = ckv.shape
    return pl.pallas_call(
        functools.partial(_cache_kv_kernel, heads=heads),
        out_shape=(jax.ShapeDtypeStruct((n, heads * HEAD_QK), BF16),
                   jax.ShapeDtypeStruct((heads * V_HEAD, n), BF16)),
        grid=(n // tm,),
        in_specs=[
            pl.BlockSpec((tm, kvl), lambda i: (i, 0)),
            pl.BlockSpec((tm, LANES), lambda i: (i, 0)),
            pl.BlockSpec((None, kvl, heads * QK_NOPE), lambda i: (li, 0, 0), pipeline_mode=pl.Buffered(1)),
            pl.BlockSpec((None, heads * V_HEAD, kvl), lambda i: (li, 0, 0), pipeline_mode=pl.Buffered(1)),
        ],
        out_specs=(pl.BlockSpec((tm, heads * HEAD_QK), lambda i: (i, 0)),
                   pl.BlockSpec((heads * V_HEAD, tm), lambda i: (0, i))),
        compiler_params=_params("parallel"),
        name="mla_cache_kv",
    )(ckv, kpe_pad, w_k, w_vt)


def _fold8(x, op):
    return op(x.reshape(x.shape[0] // SUBLANES, SUBLANES, x.shape[1]), axis=0)


def _attn_kernel(qt_ref, k_ref, vt_ref, gate_ref, o_ref, *, heads):
    for hd in range(heads):
        qk = slice(hd * HEAD_QK, (hd + 1) * HEAD_QK)
        cols = slice(hd * V_HEAD, (hd + 1) * V_HEAD)
        st = _dot(k_ref[:, qk], qt_ref[qk, :])
        pt = jnp.exp2(st - jnp.max(st, axis=0, keepdims=True))
        ot = _dot(vt_ref[cols, :], pt.astype(BF16)) / jnp.sum(pt, axis=0, keepdims=True)
        o_ref[:, cols] = (ot.T * gate_ref[:, cols].astype(F32)).astype(BF16)


def _attention(qt, k, vt, gate, *, batch, seq, heads):
    n_tok = k.shape[0]
    return pl.pallas_call(
        functools.partial(_attn_kernel, heads=heads),
        out_shape=jax.ShapeDtypeStruct((n_tok, heads * V_HEAD), BF16),
        grid=(batch,),
        in_specs=[
            pl.BlockSpec((heads * HEAD_QK, seq), lambda b: (0, b)),
            pl.BlockSpec((seq, heads * HEAD_QK), lambda b: (b, 0)),
            pl.BlockSpec((heads * V_HEAD, seq), lambda b: (0, b)),
            pl.BlockSpec((seq, heads * V_HEAD), lambda b: (b, 0)),
        ],
        out_specs=pl.BlockSpec((seq, heads * V_HEAD), lambda b: (b, 0)),
        compiler_params=_params("parallel"),
        name="mla_attention",
    )(qt, k, vt, gate)


def _attn_pipe_kernel(qt_ref, kc_ref, ko_ref, vtc_ref, vto_ref, gate_ref, o_ref, s_scr, m_scr, *, kc, nsub):
    @pl.when(pl.program_id(0) == 0)
    def _():
        s_scr[...] = jnp.zeros_like(s_scr)
        m_scr[...] = jnp.zeros_like(m_scr)

    tq = qt_ref.shape[1] // nsub
    half = tq // 2
    for u in range(nsub):
        qcols = slice(u * tq, (u + 1) * tq)
        qa = qt_ref[:, u * tq:u * tq + half]
        qb = qt_ref[:, u * tq + half:(u + 1) * tq]
        m_old = m_scr[u][0:1]
        acc = l_acc = m_acc = None
        off = 0
        for k_ref, vt_ref in ((kc_ref, vtc_ref), (ko_ref, vto_ref)):
            for c in range(k_ref.shape[0] // kc):
                rows = slice(c * kc, (c + 1) * kc)
                srows = slice(off, off + kc)
                pt = jnp.exp2(s_scr[u, srows, :] - m_old)
                lsum = _fold8(pt, jnp.sum)
                l_acc = lsum if l_acc is None else l_acc + lsum
                pv = _dot(vt_ref[:, rows], pt.astype(BF16))
                acc = pv if acc is None else acc + pv
                k_c = k_ref[rows, :]
                sa = _dot(k_c, qa)
                sb = _dot(k_c, qb)
                s_scr[u, srows, 0:half] = sa
                s_scr[u, srows, half:tq] = sb
                mx = jnp.concatenate([_fold8(sa, jnp.max), _fold8(sb, jnp.max)], axis=1)
                m_acc = mx if m_acc is None else jnp.maximum(m_acc, mx)
                off += kc
        m_scr[u] = jnp.broadcast_to(jnp.max(m_acc, axis=0, keepdims=True), (SUBLANES, tq))
        ot = acc / jnp.sum(l_acc, axis=0, keepdims=True)
        o_ref[qcols, :] = (ot.T * gate_ref[qcols, :].astype(F32)).astype(BF16)


def _attention_pipelined(qt, kc, vtc, ko, vto, gate, *, batch, seq, past, heads, tq, nsub, kc_size):
    n_tok = ko.shape[0]
    tsub, tq = tq, tq * nsub
    nq = seq // tq
    per_b = heads * nq
    jobs = batch * per_b
    jq = lambda g: jnp.minimum(g, jobs - 1)
    jp = lambda g: jnp.maximum(g - 1, 0)
    b_of = lambda j: j // per_b
    hd_of = lambda j: (j % per_b) // nq
    row_of = lambda j: b_of(j) * nq + j % nq
    return pl.pallas_call(
        functools.partial(_attn_pipe_kernel, kc=kc_size, nsub=nsub),
        out_shape=jax.ShapeDtypeStruct((n_tok, heads * V_HEAD), BF16),
        grid=(jobs + 1,),
        in_specs=[
            pl.BlockSpec((HEAD_QK, tq), lambda g: (hd_of(jq(g)), row_of(jq(g)))),
            pl.BlockSpec((past, HEAD_QK), lambda g: (b_of(jq(g)), hd_of(jq(g)))),
            pl.BlockSpec((seq, HEAD_QK), lambda g: (b_of(jq(g)), hd_of(jq(g)))),
            pl.BlockSpec((V_HEAD, past), lambda g: (hd_of(jp(g)), b_of(jp(g)))),
            pl.BlockSpec((V_HEAD, seq), lambda g: (hd_of(jp(g)), b_of(jp(g)))),
            pl.BlockSpec((tq, V_HEAD), lambda g: (row_of(jp(g)), hd_of(jp(g)))),
        ],
        out_specs=pl.BlockSpec((tq, V_HEAD), lambda g: (row_of(jp(g)), hd_of(jp(g)))),
        scratch_shapes=[pltpu.VMEM((nsub, past + seq, tsub), F32), pltpu.VMEM((nsub, SUBLANES, tsub), F32)],
        compiler_params=_params("arbitrary"),
        name="mla_attention_pipelined",
    )(qt, kc, ko, vtc, vto, gate)


def _out_kernel(*refs, tn, last):
    if last:
        y1_ref, y2_ref, w1_ref, w2_ref, x_ref, g_ref, ng_ref, o_ref = refs
    else:
        y1_ref, y2_ref, w1_ref, w2_ref, x_ref, g_ref, ng_ref, sh_ref, sc_ref, o_ref, h_ref = refs
    d = x_ref.shape[1]
    y1 = y1_ref[...]
    y2 = y2_ref[...]
    g = g_ref[0]
    ss = None
    for c in range(d // tn):
        sl = slice(c * tn, (c + 1) * tn)
        acc = _dot(y1, w1_ref[:, sl]) + _dot(y2, w2_ref[:, sl])
        xn = x_ref[:, sl] + g[:, sl] * acc
        o_ref[:, sl] = xn
        part = jnp.sum(xn * xn, axis=-1, keepdims=True)
        ss = part if ss is None else ss + part
    inv = lax.rsqrt(ss / d + EPS)
    for c in range(d // tn):
        sl = slice(c * tn, (c + 1) * tn)
        normed = o_ref[:, sl] * inv * ng_ref[:, sl]
        if last:
            o_ref[:, sl] = normed
        else:
            h_ref[:, sl] = (normed * (1.0 + sc_ref[0][:, sl]) + sh_ref[0][:, sl]).astype(BF16)


def _out_proj(y1, y2, c1, c2, w_out, li, x, mod_l, next_g, mod_next, *, tm, tn, row_of):
    n_tok, d = x.shape
    kh = w_out.shape[1] // 2
    last = mod_next is None
    in_specs = [
        pl.BlockSpec((tm, kh), lambda i: (i, c1)),
        pl.BlockSpec((tm, kh), lambda i: (i, c2)),
        pl.BlockSpec((None, kh, d), lambda i: (li, 0, 0), pipeline_mode=pl.Buffered(1)),
        pl.BlockSpec((None, kh, d), lambda i: (li, 1, 0), pipeline_mode=pl.Buffered(1)),
        pl.BlockSpec((tm, d), lambda i: (i, 0)),
        pl.BlockSpec((1, 1, d), lambda i: (row_of(i * tm), 0, 2)),
        pl.BlockSpec((1, d), lambda i: (0, 0)),
    ]
    args = [y1, y2, w_out, w_out, x, mod_l, next_g]
    out_shape = [jax.ShapeDtypeStruct((n_tok, d), F32)]
    out_specs = [pl.BlockSpec((tm, d), lambda i: (i, 0))]
    if not last:
        in_specs += [pl.BlockSpec((1, 1, d), lambda i: (row_of(i * tm), 0, 0)),
                     pl.BlockSpec((1, 1, d), lambda i: (row_of(i * tm), 0, 1))]
        args += [mod_next, mod_next]
        out_shape.append(jax.ShapeDtypeStruct((n_tok, d), BF16))
        out_specs.append(pl.BlockSpec((tm, d), lambda i: (i, 0)))
    res = pl.pallas_call(
        functools.partial(_out_kernel, tn=tn, last=last),
        out_shape=tuple(out_shape),
        grid=(n_tok // tm,),
        in_specs=in_specs,
        out_specs=tuple(out_specs),
        compiler_params=_params("parallel"),
        name="out_proj",
    )(*args)
    return res[0] if last else res


def _v_ln_kernel(h_ref, wv_ref, lg_ref, lb_ref, vn_ref, vacc, mu_scr, *, nj, tnv, n):
    @pl.when(pl.program_id(0) == 0)
    def _():
        vacc[...] = jnp.zeros_like(vacc)
        mu_scr[...] = jnp.zeros_like(mu_scr)

    width = nj * tnv

    def prev_stats():
        mu = mu_scr[:, 0:1]
        sq = functools.reduce(
            jnp.add, [jnp.sum((vacc[t] - mu) * (vacc[t] - mu), axis=-1, keepdims=True) for t in range(nj)])
        return mu, lax.rsqrt(sq / width + EPS)

    def norm_prev(t, mu, inv):
        sl = slice(t * tnv, (t + 1) * tnv)
        vn_ref[:, sl] = ((vacc[t] - mu) * inv * lg_ref[:, sl] + lb_ref[:, sl]).astype(BF16)

    @pl.when(pl.program_id(0) < n)
    def _():
        mu, inv = prev_stats()
        h = h_ref[...]
        tot = None
        for t in range(nj):
            norm_prev(t, mu, inv)
            v_new = _dot(h, wv_ref[:, t * tnv:(t + 1) * tnv])
            vacc[t] = v_new
            part = jnp.sum(v_new, axis=-1, keepdims=True)
            tot = part if tot is None else tot + part
        mu_scr[...] = jnp.broadcast_to(tot / width, mu_scr.shape)

    @pl.when(pl.program_id(0) == n)
    def _():
        mu, inv = prev_stats()
        for t in range(nj):
            norm_prev(t, mu, inv)


def _v_ln(h, w_in, li, ln_g, ln_b, *, tm, tnv):
    n_tok, d = h.shape
    sgw = ln_g.shape[1]
    nj = sgw // tnv
    n = n_tok // tm
    return pl.pallas_call(
        functools.partial(_v_ln_kernel, nj=nj, tnv=tnv, n=n),
        out_shape=jax.ShapeDtypeStruct((n_tok, sgw), BF16),
        grid=(n + 1,),
        in_specs=[
            pl.BlockSpec((tm, d), lambda i: (jnp.minimum(i, n - 1), 0)),
            pl.BlockSpec((None, d, sgw), lambda i: (li, 0, 1), pipeline_mode=pl.Buffered(1)),
            pl.BlockSpec((1, sgw), lambda i: (0, 0)),
            pl.BlockSpec((1, sgw), lambda i: (0, 0)),
        ],
        out_specs=pl.BlockSpec((tm, sgw), lambda i: (jnp.maximum(i - 1, 0), 0)),
        scratch_shapes=[pltpu.VMEM((nj, tm, tnv), F32), pltpu.VMEM((tm, LANES), F32)],
        compiler_params=_params("arbitrary"),
        name="odd_v_ln",
    )(h, w_in, ln_g, ln_b)


def _sg_kernel(h_ref, wu_ref, wg_ref, vn_ref, ws_ref, bs_ref, y_ref, *, tm, chunk, gps):
    h = h_ref[...]
    gw = y_ref.shape[1] // gps
    for s in range(gps):
        cols = slice(s * gw, (s + 1) * gw)
        u = _dot(h, wu_ref[:, cols])
        g = _dot(h, wg_ref[:, cols])
        ws = ws_ref[s]
        bs = bs_ref[s]
        for c in range(tm // chunk):
            sl = slice(c * chunk, (c + 1) * chunk)
            vs = _dot(ws, vn_ref[sl, cols]) + bs
            y_ref[sl, cols] = (u[sl] * vs * _silu(g[sl])).astype(BF16)


def _spatial_gate(h, w_in, li, vn, w_s, b_s, *, tm, gps):
    n_tok, d = h.shape
    _, groups, chunk, _ = w_s.shape
    sgw = vn.shape[1]
    tn = gps * sgw // groups
    nj = groups // gps
    return pl.pallas_call(
        functools.partial(_sg_kernel, tm=tm, chunk=chunk, gps=gps),
        out_shape=jax.ShapeDtypeStruct((n_tok, sgw), BF16),
        grid=(n_tok // tm, nj),
        in_specs=[
            pl.BlockSpec((tm, d), lambda i, j: (i, 0)),
            pl.BlockSpec((None, d, tn), lambda i, j: (li, 0, j)),
            pl.BlockSpec((None, d, tn), lambda i, j: (li, 0, 2 * nj + j)),
            pl.BlockSpec((tm, tn), lambda i, j: (i, j)),
            pl.BlockSpec((None, gps, chunk, chunk), lambda i, j: (li, j, 0, 0)),
            pl.BlockSpec((None, gps, chunk, 1), lambda i, j: (li, j, 0, 0)),
        ],
        out_specs=pl.BlockSpec((tm, tn), lambda i, j: (i, j)),
        compiler_params=_params("parallel", "arbitrary"),
        name="odd_spatial_gate",
    )(h, w_in, w_in, vn, w_s, b_s)


def _rot_cols(w):
    shp = w.shape
    w4 = w.reshape(shp[:-1] + (2, 2, QK_ROPE // 4))
    return jnp.stack([-w4[..., 1, :], w4[..., 0, :]], axis=-2).reshape(shp)


def _rope_table(n):
    f32 = np.float32
    rows = n // GRID_W
    row = np.repeat(np.arange(rows, dtype=f32), GRID_W)
    col = np.tile(np.arange(GRID_W, dtype=f32), rows)
    n_freq = QK_ROPE // 4
    inv = np.power(f32(ROPE_BASE), -np.arange(n_freq, dtype=f32) / f32(n_freq)).astype(f32)
    ar = row[:, None] * inv
    ac = col[:, None] * inv
    ang = np.concatenate([ar, ar, ac, ac], axis=-1).astype(f32)
    cs = np.concatenate([np.cos(ang), np.sin(ang)], axis=-1).astype(f32)
    return jnp.asarray(cs), jnp.asarray(np.ascontiguousarray(cs.T))


def _tile(n, pref):
    return pref if n % pref == 0 else n


def _prep_weights(e_w_in, e_w_qb, e_w_kvb, e_w_out, o_w_in, o_w_s, o_w_out, cw, ql, kvl, heads):
    n_even = e_w_in.shape[0]
    o = 4 * cw + ql + kvl
    e_in = e_w_in.astype(BF16)
    w_kpe = e_in[:, :, o:o + QK_ROPE]
    n = np.arange(heads * HEAD_QK)
    hd, r = n // HEAD_QK, n % HEAD_QK
    j = r - QK_NOPE - QK_ROPE
    first = (j % (QK_ROPE // 2)) < QK_ROPE // 4
    src = np.where(j < 0, r, QK_NOPE + np.where(first, j + QK_ROPE // 4, j - QK_ROPE // 4))
    src = hd * (QK_NOPE + QK_ROPE) + src
    sign = np.where((j >= 0) & first, -1.0, 1.0).astype(np.float32)
    cols = lax.broadcasted_iota(jnp.int32, (heads * HEAD_QK, heads * (QK_NOPE + QK_ROPE)), 1)
    sel = jnp.where(cols == jnp.asarray(src, jnp.int32)[:, None], jnp.asarray(sign)[:, None], 0.0).astype(BF16)
    w_qbt = jnp.einsum("nc,lkc->lnk", sel, e_w_qb.astype(BF16), preferred_element_type=F32).astype(BF16)
    w_kvb = e_w_kvb.astype(BF16).reshape(n_even, kvl, heads, 2, V_HEAD)
    return dict(
        e_in=e_in,
        w_kpe2=jnp.concatenate([w_kpe, _rot_cols(w_kpe)], axis=-1),
        w_mg=e_in[:, :, o + QK_ROPE:],
        w_qbt=w_qbt,
        w_k=w_kvb[:, :, :, 0, :].reshape(n_even, kvl, heads * QK_NOPE),
        w_vt=w_kvb[:, :, :, 1, :].reshape(n_even, kvl, heads * V_HEAD).transpose(0, 2, 1),
        e_out=e_w_out.astype(BF16),
        o_in=o_w_in.astype(BF16),
        w_s=o_w_s.astype(BF16),
        o_out=o_w_out.astype(BF16),
    )


def _trunk(x, *, batch, seq, cond_row, cs, cache, mod, norm_g, w, e_conv_w, e_q_norm_g, e_kv_norm_g,
           o_ln_g, o_ln_b, o_b_s, final_g, heads, emit):
    n_tok, d = x.shape
    depth = mod.shape[0]
    cw = e_conv_w.shape[2]
    ql = e_q_norm_g.shape[1]
    kvl = e_kv_norm_g.shape[1]
    assert seq & (seq - 1) == 0, "sequence length must be a power of two"
    cs, cst = (None, None) if cs is None else cs
    assert heads % HEAD_GROUP == 0 and (4 * cw) % ql == 0 and (4 * cw + ql) % kvl == 0 and cw == heads * V_HEAD
    tm = _tile(n_tok, 512)
    tml = _tile(n_tok, 1024)
    scale = float((QK_NOPE + QK_ROPE) ** -0.5 * math.log2(math.e))
    row_of = cond_row
    mods = [mod[l].reshape(mod.shape[1], 1, mod.shape[2]) for l in range(depth)]
    b_s = o_b_s[..., None]
    ckvs, kpes = [], []
    h = _first_h(x, mods[0], norm_g[0][None, :], tm=tm, row_of=row_of)
    for l in range(depth):
        i = l // 2
        if l % 2 == 0:
            y1 = _conv_branch(h, w["e_in"], i, e_conv_w[i], tm=tml, tn=_tile(cw, 512), seq=seq,
                              nsub=2 if _tile(cw, 512) % 512 == 0 else 1)
            q, gate = _q_proj(h, w["e_in"], i, (4 * cw) // ql, e_q_norm_g[i][None, :], w["w_qbt"],
                              w["w_mg"][i], cst, tm=tml, seq=seq, heads=heads, scale=scale)
            kv = _kv_proj(h, w["e_in"], i, (4 * cw + ql) // kvl, e_kv_norm_g[i][None, :], w["w_kpe2"][i],
                          w["w_k"], w["w_vt"], cs, tm=tml, seq=seq, heads=heads, emit=emit)
            if emit:
                ckvs.append(kv[2])
                kpes.append(kv[3])
            if cache is None:
                y2 = _attention(q, kv[0], kv[1], gate, batch=batch, seq=seq, heads=heads)
            else:
                c_ckv, c_kpe = cache
                past = c_ckv.shape[2]
                kpe_pad = jnp.pad(c_kpe[:, i].reshape(batch * past, QK_ROPE),
                                  ((0, 0), (0, LANES - QK_ROPE))).astype(BF16)
                kc, vc = _cache_kv(c_ckv[:, i].reshape(batch * past, kvl), kpe_pad, w["w_k"], w["w_vt"], i,
                                   tm=_tile(batch * past, 512), heads=heads)
                y2 = _attention_pipelined(q, kc, vc, kv[0], kv[1], gate, batch=batch, seq=seq, past=past,
                                          heads=heads, tq=_tile(seq // 4, 512), nsub=4,
                                          kc_size=_tile(math.gcd(past, seq), 256))
            c2, w_out = 0, w["e_out"]
        else:
            vn = _v_ln(h, w["o_in"], i, o_ln_g[i][None, :], o_ln_b[i][None, :], tm=tm,
                       tnv=_tile(o_ln_g.shape[1], 256))
            groups = w["w_s"].shape[1]
            y1 = y2 = _spatial_gate(h, w["o_in"], i, vn, w["w_s"], b_s, tm=tml, gps=4 if groups % 4 == 0 else 1)
            c2, w_out = 1, w["o_out"]
        if l + 1 < depth:
            x, h = _out_proj(y1, y2, 0, c2, w_out, i, x, mods[l], norm_g[l + 1][None, :], mods[l + 1],
                             tm=tm, tn=_tile(d, 512), row_of=row_of)
        else:
            x = _out_proj(y1, y2, 0, c2, w_out, i, x, mods[l], final_g[None, :], None,
                          tm=tm, tn=_tile(d, 512), row_of=row_of)
    return x, ckvs, kpes


def kernel(x_prompt, x_sample, cache_ckv, cache_kpe, c, c_ctx, norm_g, w_ada, b_ada, e_w_in, e_conv_w,
           e_q_norm_g, e_w_qb, e_kv_norm_g, e_w_kvb, e_w_out, o_w_in, o_ln_g, o_ln_b, o_w_s, o_b_s, o_w_out,
           final_g):
    batch, seq, d = x_prompt.shape
    dec_batch, dec_seq, _ = x_sample.shape
    cw = e_conv_w.shape[2]
    ql = e_q_norm_g.shape[1]
    kvl = e_kv_norm_g.shape[1]
    heads = e_w_qb.shape[2] // (QK_NOPE + QK_ROPE)

    rows = -(-(dec_batch + 1) // SUBLANES) * SUBLANES
    cond = jnp.concatenate([c, c_ctx[None, :], jnp.zeros((rows - dec_batch - 1, d), F32)], axis=0)
    mod = _ada(cond, w_ada, b_ada)

    w = _prep_weights(e_w_in, e_w_qb, e_w_kvb, e_w_out, o_w_in, o_w_s, o_w_out, cw, ql, kvl, heads)
    shared = dict(mod=mod, norm_g=norm_g, w=w, e_conv_w=e_conv_w, e_q_norm_g=e_q_norm_g,
                  e_kv_norm_g=e_kv_norm_g, o_ln_g=o_ln_g, o_ln_b=o_ln_b, o_b_s=o_b_s,
                  final_g=final_g, heads=heads)

    y_prompt, ckvs, kpes = _trunk(x_prompt.reshape(batch * seq, d), batch=batch, seq=seq,
                                  cond_row=lambda r: dec_batch, cs=None, cache=None, emit=True, **shared)
    y_sample, _, _ = _trunk(x_sample.reshape(dec_batch * dec_seq, d), batch=dec_batch, seq=dec_seq,
                            cond_row=lambda r: r // dec_seq, cs=_rope_table(dec_seq),
                            cache=(cache_ckv, cache_kpe), emit=False, **shared)
    new_ckv = jnp.stack([t.reshape(batch, seq, kvl) for t in ckvs], axis=1)
    new_kpe = jnp.stack([t.reshape(batch, seq, QK_ROPE) for t in kpes], axis=1)
    return (y_prompt.reshape(batch, seq, d), y_sample.reshape(dec_batch, dec_seq, d), new_ckv, new_kpe)
```
